```python
import math
import jax
import jax.numpy as jnp
from jax import lax
import numpy as np

D_MODEL = 1024
BATCH = 4
SEQ = 8192
DEPTH = 1
DEC_BATCH = 32
DEC_SEQ = 64
PAST_LEN = 2048

CHUNK = 64
QBLOCK = 128
SB_HEADS = 8
SB_DIM = 64
SB_W = SB_HEADS * SB_DIM
DF_HEADS = 4
DF_DIM = 64
DF_VDIM = 2 * DF_DIM
DF_W = DF_HEADS * 2 * DF_DIM
DF_VW = DF_HEADS * DF_VDIM
MEM_TOKENS = 256
MEM_HEADS = 4
MEM_DIM = D_MODEL // MEM_HEADS
N_GROUPS = 4
EXPERTS_PER_GROUP = 8
N_EXPERTS = N_GROUPS * EXPERTS_PER_GROUP
TOP_K = 2
D_EXPERT = 512
EBLOCK = 256
ROPE_THETA = 10000.0
LN_EPS = 1e-5
RMS_EPS = 1e-5
DN_ALPHA = (2 * DEPTH) ** 0.25
DN_BETA = (8 * DEPTH) ** -0.25
IN_SIZES = (SB_W, SB_W, SB_W, DF_W, DF_W, DF_VW, D_MODEL, D_MODEL)

kernel_name = 'stickbreak_diffattn_hmoe_streaming_encoder'


def _split_offsets():
    return [int(o) for o in np.cumsum(IN_SIZES)[:-1]]


def _layer_norm(x, g, b):
    xf = x.astype(jnp.float32)
    mu = jnp.mean(xf, axis=-1, keepdims=True)
    var = jnp.mean(jnp.square(xf - mu), axis=-1, keepdims=True)
    return ((xf - mu) * lax.rsqrt(var + LN_EPS) * g.astype(jnp.float32) + b.astype(jnp.float32)).astype(x.dtype)


def _rms_norm(x, g):
    xf = x.astype(jnp.float32)
    xf = xf * lax.rsqrt(jnp.mean(jnp.square(xf), axis=-1, keepdims=True) + RMS_EPS)
    return (xf * g.astype(jnp.float32)).astype(x.dtype)


def _rope(x, pos):
    half = x.shape[-1] // 2
    inv_freq = jnp.exp(jnp.arange(half, dtype=jnp.float32) * (-math.log(ROPE_THETA) / half))
    ang = pos.astype(jnp.float32)[:, None] * inv_freq[None, :]
    ang = ang.reshape((1, pos.shape[0]) + (1,) * (x.ndim - 3) + (half,))
    cos, sin = jnp.cos(ang), jnp.sin(ang)
    x1 = x[..., :half].astype(jnp.float32)
    x2 = x[..., half:].astype(jnp.float32)
    return jnp.concatenate([x1 * cos - x2 * sin, x1 * sin + x2 * cos], axis=-1).astype(x.dtype)


def _sweep(fn, q, q_pos):
    B, S = q.shape[:2]
    if S <= QBLOCK or S % QBLOCK:
        return fn(q, q_pos)
    nb = S // QBLOCK
    qb = jnp.moveaxis(q.reshape((B, nb, QBLOCK) + q.shape[2:]), 1, 0)
    out = lax.map(lambda a: fn(a[0], a[1]), (qb, q_pos.reshape(nb, QBLOCK)))
    return jnp.moveaxis(out, 0, 1).reshape((B, S) + out.shape[3:])


def _stick_breaking_block(q, q_pos, k, v, k_pos):
    z = jnp.einsum('bqhd,bkhd->bhqk', q, k, preferred_element_type=jnp.float32) * (SB_DIM ** -0.5)
    mask = k_pos[None, :] < q_pos[:, None]
    log_om = jnp.where(mask, -jax.nn.softplus(z), 0.0)
    between = lax.cumsum(log_om, axis=3, reverse=True) - log_om
    a = jnp.where(mask, jnp.exp(jax.nn.log_sigmoid(z) + between), 0.0)
    return jnp.einsum('bhqk,bkhd->bqhd', a.astype(v.dtype), v)


def _diff_block(q, q_pos, k, v, k_pos, lam):
    s = jnp.einsum('bqhmd,bkhmd->bmhqk', q, k, preferred_element_type=jnp.float32) * (DF_DIM ** -0.5)
    mask = (k_pos[None, :] // CHUNK) <= (q_pos[:, None] // CHUNK)
    p = jax.nn.softmax(jnp.where(mask, s, -jnp.inf), axis=-1)
    w = p[:, 0] - lam * p[:, 1]
    return jnp.einsum('bhqk,bkhe->bqhe', w.astype(v.dtype), v)


def _memory_attention(x, mem_k, mem_v, w_mq, w_mo):
    B, S, _ = x.shape
    q = (x @ w_mq).reshape(B, S, MEM_HEADS, MEM_DIM)
    s = jnp.einsum('bqhd,bmhd->bhqm', q, mem_k, preferred_element_type=jnp.float32) * (MEM_DIM ** -0.5)
    p = jax.nn.softmax(s, axis=-1)
    o = jnp.einsum('bhqm,bmhd->bqhd', p.astype(mem_v.dtype), mem_v).reshape(B, S, D_MODEL)
    return o @ w_mo


def _grouped_swiglu(xf, expert, gate, w_up, w_gate, w_down):
    N, D = xf.shape
    E = w_up.shape[0]
    K = expert.shape[1]
    A = N * K
    flat_e = expert.reshape(-1)
    order = jnp.argsort(flat_e)
    sorted_e = flat_e[order]
    counts = jnp.bincount(flat_e, length=E)
    padded = (counts + EBLOCK - 1) // EBLOCK * EBLOCK
    pad_end = jnp.cumsum(padded)
    pad_start = pad_end - padded
    start = jnp.cumsum(counts) - counts
    dest = pad_start[sorted_e] + jnp.arange(A, dtype=jnp.int32) - start[sorted_e]
    n_blocks = (A + E * (EBLOCK - 1) + EBLOCK - 1) // EBLOCK
    rows = n_blocks * EBLOCK
    tok = jnp.zeros((rows,), jnp.int32).at[dest].set((order // K).astype(jnp.int32))
    wgt = jnp.zeros((rows,), jnp.float32).at[dest].set(gate.reshape(-1)[order])
    blk_e = jnp.minimum(jnp.searchsorted(pad_end, jnp.arange(n_blocks) * EBLOCK, side='right'), E - 1)
    xb = xf[tok].reshape(n_blocks, EBLOCK, D)

    def expert_block(args):
        xblk, e = args
        hid = jax.nn.silu(xblk @ w_gate[e]) * (xblk @ w_up[e])
        return hid @ w_down[e]

    yb = lax.map(expert_block, (xb, blk_e)).reshape(rows, D)
    return jnp.zeros_like(xf).at[tok].add(yb * wgt[:, None].astype(yb.dtype))


def _hier_moe(x, w_group, b_group, w_router, b_router, w_up, w_gate, w_down):
    B, S, D = x.shape
    xf = x.reshape(B * S, D)
    N = xf.shape[0]
    rows = jnp.arange(N)
    g_logits = jnp.dot(xf, w_group, preferred_element_type=jnp.float32) + b_group.astype(jnp.float32)
    grp = jnp.argmax(g_logits, axis=-1).astype(jnp.int32)
    g_prob = jax.nn.softmax(g_logits, axis=-1)[rows, grp][:, None]
    e_logits = (jnp.dot(xf, w_router, preferred_element_type=jnp.float32) + b_router.astype(jnp.float32))
    e_logits = e_logits.reshape(N, N_GROUPS, EXPERTS_PER_GROUP)[rows, grp]
    top_v, top_i = lax.top_k(e_logits, TOP_K)
    gate = jax.nn.softmax(top_v, axis=-1) * g_prob
    expert = grp[:, None] * EXPERTS_PER_GROUP + top_i.astype(jnp.int32)
    return _grouped_swiglu(xf, expert, gate, w_up, w_gate, w_down).reshape(B, S, D)


def _layer(x, past_sb_k, past_sb_v, past_df_k, past_df_v, mem_k, mem_v, lam_init,
           w_in, lam_q1, lam_k1, lam_q2, lam_k2, subln_g, w_branch_a, w_branch_b, w_out,
           ln1_g, ln1_b, w_mq, w_mo, ln2_g, ln2_b, w_group, b_group, w_router, b_router,
           w_up, w_gate, w_down, ln3_g, ln3_b):
    B, S, _ = x.shape
    P = past_sb_k.shape[1]
    q_pos = P + jnp.arange(S, dtype=jnp.int32)
    k_pos = jnp.arange(P + S, dtype=jnp.int32)
    h = x @ w_in
    qa, ka, va, qb, kb, vb, ga, gb = jnp.split(h, _split_offsets(), axis=-1)
    qa = qa.reshape(B, S, SB_HEADS, SB_DIM)
    ka = ka.reshape(B, S, SB_HEADS, SB_DIM)
    va = va.reshape(B, S, SB_HEADS, SB_DIM)
    qb = _rope(qb.reshape(B, S, DF_HEADS, 2, DF_DIM), q_pos)
    kb = _rope(kb.reshape(B, S, DF_HEADS, 2, DF_DIM), q_pos).reshape(B, S, DF_HEADS, 2 * DF_DIM)
    vb = vb.reshape(B, S, DF_HEADS, DF_VDIM)
    ka_all = jnp.concatenate([past_sb_k, ka], axis=1)
    va_all = jnp.concatenate([past_sb_v, va], axis=1)
    kb_all = jnp.concatenate([past_df_k, kb], axis=1).reshape(B, P + S, DF_HEADS, 2, DF_DIM)
    vb_all = jnp.concatenate([past_df_v, vb], axis=1)
    oa = _sweep(lambda qq, pp: _stick_breaking_block(qq, pp, ka_all, va_all, k_pos), qa, q_pos)
    f32 = jnp.float32
    lam = (jnp.exp(jnp.sum(lam_q1.astype(f32) * lam_k1.astype(f32)))
           - jnp.exp(jnp.sum(lam_q2.astype(f32) * lam_k2.astype(f32))) + lam_init)
    ob = _sweep(lambda qq, pp: _diff_block(qq, pp, kb_all, vb_all, k_pos, lam), qb, q_pos)
    ob = _rms_norm(ob, subln_g) * (1.0 - lam_init)
    ya = oa.reshape(B, S, SB_W) @ w_branch_a
    yb = ob.reshape(B, S, DF_VW) @ w_branch_b
    mix = (jax.nn.sigmoid(ga) * ya + jax.nn.sigmoid(gb) * yb) @ w_out
    x = _layer_norm(DN_ALPHA * x + mix, ln1_g, ln1_b)
    x = _layer_norm(DN_ALPHA * x + _memory_attention(x, mem_k, mem_v, w_mq, w_mo), ln2_g, ln2_b)
    x = _layer_norm(DN_ALPHA * x + _hier_moe(x, w_group, b_group, w_router, b_router, w_up, w_gate, w_down), ln3_g, ln3_b)
    return x, ka, va, kb, vb


def setup_inputs(seed: int = 0) -> dict:
    key = jax.random.key(seed)
    ks = jax.random.split(key, 40)
    L, D = DEPTH, D_MODEL

    def nrm(k, shape, scale):
        return jax.random.normal(k, shape, jnp.float32) * scale

    win_keys = jax.random.split(ks[0], len(IN_SIZES))
    in_scales = (1.0, 1.0, DN_BETA, 1.0, 1.0, DN_BETA, 1.0, 1.0)
    w_in = jnp.concatenate([nrm(k, (L, D, n), sc * D ** -0.5)
                            for k, n, sc in zip(win_keys, IN_SIZES, in_scales)], axis=-1)
    return {
        'x_prompt': nrm(ks[1], (BATCH, SEQ, D), 1.0),
        'mem_prompt': nrm(ks[2], (BATCH, MEM_TOKENS, D), 1.0),
        'x_sample': nrm(ks[3], (DEC_BATCH, DEC_SEQ, D), 1.0),
        'cache_sb_k': nrm(ks[4], (L, DEC_BATCH, PAST_LEN, SB_HEADS, SB_DIM), 1.0),
        'cache_sb_v': nrm(ks[5], (L, DEC_BATCH, PAST_LEN, SB_HEADS, SB_DIM), DN_BETA),
        'cache_diff_k': nrm(ks[6], (L, DEC_BATCH, PAST_LEN, DF_HEADS, 2 * DF_DIM), 1.0),
        'cache_diff_v': nrm(ks[7], (L, DEC_BATCH, PAST_LEN, DF_HEADS, DF_VDIM), DN_BETA),
        'cache_mem_k': nrm(ks[8], (L, DEC_BATCH, MEM_TOKENS, MEM_HEADS, MEM_DIM), 1.0),
        'cache_mem_v': nrm(ks[9], (L, DEC_BATCH, MEM_TOKENS, MEM_HEADS, MEM_DIM), DN_BETA),
        'w_in': w_in,
        'lam_q1': nrm(ks[10], (L, DF_DIM), 0.1),
        'lam_k1': nrm(ks[11], (L, DF_DIM), 0.1),
        'lam_q2': nrm(ks[12], (L, DF_DIM), 0.1),
        'lam_k2': nrm(ks[13], (L, DF_DIM), 0.1),
        'subln_g': 1.0 + nrm(ks[14], (L, DF_VDIM), 0.02),
        'w_branch_a': nrm(ks[15], (L, SB_W, D), DN_BETA * SB_W ** -0.5),
        'w_branch_b': nrm(ks[16], (L, DF_VW, D), DN_BETA * DF_VW ** -0.5),
        'w_out': nrm(ks[17], (L, D, D), DN_BETA * D ** -0.5),
        'ln1_g': 1.0 + nrm(ks[18], (L, D), 0.02),
        'ln1_b': nrm(ks[19], (L, D), 0.02),
        'w_mq': nrm(ks[20], (L, D, D), D ** -0.5),
        'w_mk': nrm(ks[21], (L, D, D), D ** -0.5),
        'w_mv': nrm(ks[22], (L, D, D), DN_BETA * D ** -0.5),
        'w_mo': nrm(ks[23], (L, D, D), DN_BETA * D ** -0.5),
        'ln2_g': 1.0 + nrm(ks[24], (L, D), 0.02),
        'ln2_b': nrm(ks[25], (L, D), 0.02),
        'w_group': nrm(ks[26], (L, D, N_GROUPS), D ** -0.5),
        'b_group': nrm(ks[27], (L, N_GROUPS), 0.01),
        'w_router': nrm(ks[28], (L, D, N_EXPERTS), D ** -0.5),
        'b_router': nrm(ks[29], (L, N_EXPERTS), 0.01),
        'w_up': nrm(ks[30], (L, N_EXPERTS, D, D_EXPERT), DN_BETA * D ** -0.5),
        'w_gate': nrm(ks[31], (L, N_EXPERTS, D, D_EXPERT), D ** -0.5),
        'w_down': nrm(ks[32], (L, N_EXPERTS, D_EXPERT, D), DN_BETA * D_EXPERT ** -0.5),
        'ln3_g': 1.0 + nrm(ks[33], (L, D), 0.02),
        'ln3_b': nrm(ks[34], (L, D), 0.02),
    }


def reference(x_prompt, mem_prompt, x_sample, cache_sb_k, cache_sb_v, cache_diff_k, cache_diff_v,
              cache_mem_k, cache_mem_v, w_in, lam_q1, lam_k1, lam_q2, lam_k2, subln_g,
              w_branch_a, w_branch_b, w_out, ln1_g, ln1_b, w_mq, w_mk, w_mv, w_mo, ln2_g, ln2_b,
              w_group, b_group, w_router, b_router, w_up, w_gate, w_down, ln3_g, ln3_b):
    yp, ys = x_prompt, x_sample
    Bp = x_prompt.shape[0]
    n_mem = mem_prompt.shape[1]
    sbk_p, sbv_p, dfk_p, dfv_p, mk_p, mv_p = [], [], [], [], [], []
    sbk_s, sbv_s, dfk_s, dfv_s = [], [], [], []
    for l in range(DEPTH):
        lam_init = 0.8 - 0.6 * math.exp(-0.3 * l)
        lw = (w_in[l], lam_q1[l], lam_k1[l], lam_q2[l], lam_k2[l], subln_g[l], w_branch_a[l], w_branch_b[l],
              w_out[l], ln1_g[l], ln1_b[l], w_mq[l], w_mo[l], ln2_g[l], ln2_b[l], w_group[l], b_group[l],
              w_router[l], b_router[l], w_up[l], w_gate[l], w_down[l], ln3_g[l], ln3_b[l])
        mk = (mem_prompt @ w_mk[l]).reshape(Bp, n_mem, MEM_HEADS, MEM_DIM)
        mv = (mem_prompt @ w_mv[l]).reshape(Bp, n_mem, MEM_HEADS, MEM_DIM)
        empty_sb = jnp.zeros((Bp, 0, SB_HEADS, SB_DIM), x_prompt.dtype)
        empty_df = jnp.zeros((Bp, 0, DF_HEADS, 2 * DF_DIM), x_prompt.dtype)
        yp, ka, va, kb, vb = _layer(yp, empty_sb, empty_sb, empty_df, empty_df, mk, mv, lam_init, *lw)
        sbk_p.append(ka); sbv_p.append(va); dfk_p.append(kb); dfv_p.append(vb)
        mk_p.append(mk); mv_p.append(mv)
        ys, ka_s, va_s, kb_s, vb_s = _layer(ys, cache_sb_k[l], cache_sb_v[l], cache_diff_k[l], cache_diff_v[l],
                                            cache_mem_k[l], cache_mem_v[l], lam_init, *lw)
        sbk_s.append(ka_s); sbv_s.append(va_s); dfk_s.append(kb_s); dfv_s.append(vb_s)
    return (yp, ys, jnp.stack(sbk_p), jnp.stack(sbv_p), jnp.stack(dfk_p), jnp.stack(dfv_p),
            jnp.stack(mk_p), jnp.stack(mv_p), jnp.stack(sbk_s), jnp.stack(sbv_s),
            jnp.stack(dfk_s), jnp.stack(dfv_s))
```

```python
import functools
import math

import jax
import jax.numpy as jnp
import numpy as np
from jax import lax
from jax.experimental import pallas as pl
from jax.experimental.pallas import tpu as pltpu

F32 = jnp.float32
BF16 = jnp.bfloat16

D_MODEL = 1024
SB_HEADS = 8
SB_DIM = 64
SB_W = SB_HEADS * SB_DIM
DF_HEADS = 4
DF_DIM = 64
DF_W = DF_HEADS * 2 * DF_DIM
CHUNK = 64
MEM_HEADS = 4
MEM_DIM = D_MODEL // MEM_HEADS
N_GROUPS = 4
EXPERTS_PER_GROUP = 8
N_EXPERTS = N_GROUPS * EXPERTS_PER_GROUP
TOP_K = 2
D_EXPERT = 512
EBLOCK = 256
ROPE_THETA = 10000.0
LN_EPS = 1e-5
RMS_EPS = 1e-5
DEPTH = 1
DN_ALPHA = (2 * DEPTH) ** 0.25
IN_W = 3 * SB_W + 3 * DF_W + 2 * D_MODEL

LANES = 128
HEAD_GROUPS = 4
NEG_BIG = -1e30
VMEM_LIMIT = 48 * 1024 * 1024


def _dot(a, b):
    return jnp.dot(a, b, preferred_element_type=F32)


def _dot_nt(a, b):
    return lax.dot_general(a, b, (((1,), (1,)), ((), ())), preferred_element_type=F32)


def _layer_norm(h, g, b):
    mu = jnp.mean(h, axis=-1, keepdims=True)
    d = h - mu
    var = jnp.mean(d * d, axis=-1, keepdims=True)
    return d * lax.rsqrt(var + LN_EPS) * g + b


def _params(sem, vmem=VMEM_LIMIT):
    return pltpu.CompilerParams(dimension_semantics=sem, vmem_limit_bytes=vmem)


def _inproj_kernel(x_ref, w_ref, cos_ref, sina_ref, sinb_ref,
                   ka_ref, va_ref, kb_ref, vb_ref, sga_ref, sgb_ref,
                   qa16_ref, ka16_ref, va16_ref, qb16_ref, kb16_ref, vb16_ref):
    xb = x_ref[...].astype(BF16)

    def mm(off, width):
        return _dot(xb, w_ref[:, off:off + width])

    def put_groups(ref, val):
        for g in range(HEAD_GROUPS):
            ref[g] = val[:, g * LANES:(g + 1) * LANES].astype(BF16)

    def rope(t):
        n = t.shape[1]
        reps = n // LANES
        cos = jnp.concatenate([cos_ref[...]] * reps, axis=1)
        sina = jnp.concatenate([sina_ref[...]] * reps, axis=1)
        sinb = jnp.concatenate([sinb_ref[...]] * reps, axis=1)
        fwd = pltpu.roll(t, n - DF_DIM // 2, 1)
        bwd = pltpu.roll(t, DF_DIM // 2, 1)
        return t * cos + fwd * sina + bwd * sinb

    qa = mm(0, SB_W)
    put_groups(qa16_ref, qa * (SB_DIM ** -0.5))
    ka = mm(SB_W, SB_W)
    ka_ref[...] = ka
    put_groups(ka16_ref, ka)
    va = mm(2 * SB_W, SB_W)
    va_ref[...] = va
    put_groups(va16_ref, va)
    off = 3 * SB_W
    qb = rope(mm(off, DF_W))
    put_groups(qb16_ref, qb * (DF_DIM ** -0.5))
    kb = rope(mm(off + DF_W, DF_W))
    kb_ref[...] = kb
    put_groups(kb16_ref, kb)
    vb = mm(off + 2 * DF_W, DF_W)
    vb_ref[...] = vb
    put_groups(vb16_ref, vb)
    off = off + 3 * DF_W
    sga_ref[...] = jax.nn.sigmoid(mm(off, D_MODEL))
    sgb_ref[...] = jax.nn.sigmoid(mm(off + D_MODEL, D_MODEL))


def _rope_tables(pos0, seq):
    half = DF_DIM // 2
    inv_freq = jnp.exp(jnp.arange(half, dtype=F32) * (-math.log(ROPE_THETA) / half))
    ang = (pos0 + jnp.arange(seq, dtype=jnp.int32)).astype(F32)[:, None] * inv_freq[None, :]
    cos, sin = jnp.cos(ang), jnp.sin(ang)
    zero = jnp.zeros_like(sin)
    reps = LANES // DF_DIM
    cos_t = jnp.tile(jnp.concatenate([cos, cos], axis=1), (1, reps))
    sina_t = jnp.tile(jnp.concatenate([-sin, zero], axis=1), (1, reps))
    sinb_t = jnp.tile(jnp.concatenate([zero, sin], axis=1), (1, reps))
    return cos_t, sina_t, sinb_t


def _in_projection(x, w_in16, pos0):
    B, S, D = x.shape
    N = B * S
    tm = min(256, S)
    spb = S // tm
    xf = x.reshape(N, D)
    cos_t, sina_t, sinb_t = _rope_tables(pos0, S)
    row = lambda i: (i, 0)
    tab = lambda i: (i % spb, 0)
    grp = lambda i: (i // spb, 0, i % spb, 0)
    f32_512 = jax.ShapeDtypeStruct((N, SB_W), F32)
    f32_1024 = jax.ShapeDtypeStruct((N, D), F32)
    g16 = jax.ShapeDtypeStruct((B, HEAD_GROUPS, S, LANES), BF16)
    spec512 = pl.BlockSpec((tm, SB_W), row)
    spec1024 = pl.BlockSpec((tm, D), row)
    gspec = pl.BlockSpec((None, HEAD_GROUPS, tm, LANES), grp)
    tspec = pl.BlockSpec((tm, LANES), tab)
    return pl.pallas_call(
        _inproj_kernel,
        grid=(N // tm,),
        in_specs=[spec1024, pl.BlockSpec((D, IN_W), lambda i: (0, 0)), tspec, tspec, tspec],
        out_specs=[spec512, spec512, spec512, spec512, spec1024, spec1024,
                   gspec, gspec, gspec, gspec, gspec, gspec],
        out_shape=[f32_512, f32_512, f32_512, f32_512, f32_1024, f32_1024,
                   g16, g16, g16, g16, g16, g16],
        compiler_params=_params(("parallel",), 56 * 1024 * 1024),
        name="in_projection",
    )(xf, w_in16, cos_t, sina_t, sinb_t)


def _pack_kernel(c_ref, n_ref, o_ref, *, n_cache_blocks, new_rows):
    j = pl.program_id(1)

    @pl.when(j < n_cache_blocks)
    def _():
        c = c_ref[...]
        for g in range(HEAD_GROUPS):
            o_ref[g] = c[:, g * LANES:(g + 1) * LANES].astype(BF16)

    @pl.when(j == n_cache_blocks)
    def _():
        o_ref[...] = jnp.zeros(o_ref.shape, BF16)
        o_ref[:, 0:new_rows, :] = n_ref[...]


def _pack_keys(cache, new16, tk):
    B, P, W = cache.shape
    S = new16.shape[2]
    ncb = P // tk
    return pl.pallas_call(
        functools.partial(_pack_kernel, n_cache_blocks=ncb, new_rows=S),
        grid=(B, ncb + 1),
        in_specs=[pl.BlockSpec((None, tk, W), lambda b, j: (b, jnp.minimum(j, ncb - 1), 0)),
                  pl.BlockSpec((None, HEAD_GROUPS, S, LANES), lambda b, j: (b, 0, 0, 0))],
        out_specs=pl.BlockSpec((None, HEAD_GROUPS, tk, LANES), lambda b, j: (b, 0, j, 0)),
        out_shape=jax.ShapeDtypeStruct((B, HEAD_GROUPS, P + tk, LANES), BF16),
        compiler_params=_params(("parallel", "arbitrary")),
        name="pack_keys",
    )(cache, new16)


def _schedule(n_q, tq, tk, pos0):
    qi, kb, first, last = [], [], [], []
    for i in range(n_q):
        top = -(-(pos0 + (i + 1) * tq) // tk) - 1
        for j in range(top, -1, -1):
            qi.append(i)
            kb.append(j)
            first.append(1 if j == top else 0)
            last.append(1 if j == 0 else 0)
    mk = lambda v: jnp.asarray(np.asarray(v, np.int32))
    return mk(qi), mk(kb), mk(first), mk(last)


def _half_masks():
    lane = lax.broadcasted_iota(jnp.int32, (1, LANES), 1)
    lo = jnp.where(lane < LANES // 2, 1.0, 0.0).astype(BF16)
    hi = jnp.where(lane >= LANES // 2, 1.0, 0.0).astype(BF16)
    return lo, hi


def _sb_kernel(qi_ref, kb_ref, first_ref, last_ref, q_ref, k_ref, v_ref, tri_ref, o_ref,
               acc_ref, carry_ref, *, tq, tk, pos0):
    s = pl.program_id(1)
    qi = qi_ref[s]
    kb = kb_ref[s]

    @pl.when(first_ref[s] == 1)
    def _():
        acc_ref[...] = jnp.zeros(acc_ref.shape, F32)
        carry_ref[...] = jnp.zeros(carry_ref.shape, F32)

    q_pos = pos0 + qi * tq + lax.broadcasted_iota(jnp.int32, (tq, tk), 0)
    k_pos = kb * tk + lax.broadcasted_iota(jnp.int32, (tq, tk), 1)
    mask = k_pos < q_pos
    half = _half_masks()
    tri = tri_ref[...]
    reps = tk // LANES

    def group(g, c):
        q = q_ref[g]
        k = k_ref[g]
        v = v_ref[g]
        upd = jnp.zeros((tq, LANES), F32)
        for j in range(2):
            z = _dot_nt(q * half[j], k)
            sp = jnp.maximum(z, 0.0) + jnp.log(1.0 + jnp.exp(-jnp.abs(z)))
            sp = jnp.where(mask, sp, 0.0)
            sp_hi = sp.astype(BF16)
            sp_lo = (sp - sp_hi.astype(F32)).astype(BF16)
            suffix = _dot(sp_hi, tri) + _dot(sp_lo, tri)
            carry = carry_ref[2 * g + j]
            total = suffix + jnp.concatenate([carry] * reps, axis=1)
            a = jnp.where(mask, jnp.exp(z - total), 0.0).astype(BF16)
            upd = upd + _dot(a, v * half[j])
            carry_ref[2 * g + j] = carry + jnp.broadcast_to(suffix[:, 0:1], (tq, LANES))
        acc_ref[g] = acc_ref[g] + upd
        return c

    lax.fori_loop(0, HEAD_GROUPS, group, 0)

    @pl.when(last_ref[s] == 1)
    def _():
        for g in range(HEAD_GROUPS):
            o_ref[:, g * LANES:(g + 1) * LANES] = acc_ref[g].astype(BF16)


def _tri_ones(tk):
    r = np.arange(tk)
    return jnp.asarray((r[:, None] >= r[None, :]).astype(np.float32)).astype(BF16)


def _sb_attention(q16, k16, v16, *, tq, tk, pos0):
    B, _, S, _ = q16.shape
    n_q = S // tq
    sched = _schedule(n_q, tq, tk, pos0)
    steps = int(sched[0].shape[0])
    qspec = pl.BlockSpec((None, HEAD_GROUPS, tq, LANES), lambda b, s, qi, kb, f, l: (b, 0, qi[s], 0))
    kspec = pl.BlockSpec((None, HEAD_GROUPS, tk, LANES), lambda b, s, qi, kb, f, l: (b, 0, kb[s], 0))
    grid_spec = pltpu.PrefetchScalarGridSpec(
        num_scalar_prefetch=4,
        grid=(B, steps),
        in_specs=[qspec, kspec, kspec,
                  pl.BlockSpec((tk, tk), lambda b, s, qi, kb, f, l: (0, 0))],
        out_specs=pl.BlockSpec((tq, SB_W), lambda b, s, qi, kb, f, l: (b * n_q + qi[s], 0)),
        scratch_shapes=[pltpu.VMEM((HEAD_GROUPS, tq, LANES), F32),
                        pltpu.VMEM((SB_HEADS, tq, LANES), F32)],
    )
    return pl.pallas_call(
        functools.partial(_sb_kernel, tq=tq, tk=tk, pos0=pos0),
        grid_spec=grid_spec,
        out_shape=jax.ShapeDtypeStruct((B * S, SB_W), BF16),
        compiler_params=_params(("parallel", "arbitrary")),
        name="stick_breaking_attention",
    )(*sched, q16, k16, v16, _tri_ones(tk))


def _df_kernel(qi_ref, kb_ref, first_ref, last_ref, q_ref, k_ref, v_ref, lam_ref, g_ref, o_ref,
               m_ref, l_ref, acc_ref, *, tq, tk, pos0, lam_init):
    s = pl.program_id(1)
    qi = qi_ref[s]
    kb = kb_ref[s]

    @pl.when(first_ref[s] == 1)
    def _():
        m_ref[...] = jnp.full(m_ref.shape, NEG_BIG, F32)
        l_ref[...] = jnp.zeros(l_ref.shape, F32)
        acc_ref[...] = jnp.zeros(acc_ref.shape, F32)

    q_chunk = (pos0 + qi * tq + lax.broadcasted_iota(jnp.int32, (tq, tk), 0)) // CHUNK
    k_chunk = (kb * tk + lax.broadcasted_iota(jnp.int32, (tq, tk), 1)) // CHUNK
    mask = k_chunk <= q_chunk
    half = _half_masks()
    reps = tk // LANES

    def head(h, c):
        q = q_ref[h]
        k = k_ref[h]
        v = v_ref[h]
        for j in range(2):
            idx = 2 * h + j
            sc = jnp.where(mask, _dot_nt(q * half[j], k), NEG_BIG)
            m_old = m_ref[idx]
            m_new = jnp.maximum(m_old, jnp.max(sc, axis=1, keepdims=True))
            alpha = jnp.exp(m_old - m_new)
            p = jnp.exp(sc - jnp.concatenate([m_new] * reps, axis=1))
            l_ref[idx] = alpha * l_ref[idx] + jnp.sum(p, axis=1, keepdims=True)
            acc_ref[idx] = alpha * acc_ref[idx] + _dot(p.astype(BF16), v)
            m_ref[idx] = m_new
        return c

    lax.fori_loop(0, DF_HEADS, head, 0)

    @pl.when(last_ref[s] == 1)
    def _():
        lv = lam_ref[...]
        lam = (jnp.exp(jnp.sum(lv[0:1] * lv[1:2], axis=1, keepdims=True))
               - jnp.exp(jnp.sum(lv[2:3] * lv[3:4], axis=1, keepdims=True)) + lam_init)
        gain = g_ref[...] * (1.0 - lam_init)
        for h in range(DF_HEADS):
            o = acc_ref[2 * h] / l_ref[2 * h] - lam * (acc_ref[2 * h + 1] / l_ref[2 * h + 1])
            o = o * lax.rsqrt(jnp.mean(o * o, axis=-1, keepdims=True) + RMS_EPS)
            o_ref[:, h * LANES:(h + 1) * LANES] = (o * gain).astype(BF16)


def _df_attention(q16, k16, v16, lam_vecs, subln_g, *, tq, tk, pos0, lam_init):
    B, _, S, _ = q16.shape
    n_q = S // tq
    sched = _schedule(n_q, tq, tk, pos0)
    steps = int(sched[0].shape[0])
    qspec = pl.BlockSpec((None, DF_HEADS, tq, LANES), lambda b, s, qi, kb, f, l: (b, 0, qi[s], 0))
    kspec = pl.BlockSpec((None, DF_HEADS, tk, LANES), lambda b, s, qi, kb, f, l: (b, 0, kb[s], 0))
    const = lambda b, s, qi, kb, f, l: (0, 0)
    grid_spec = pltpu.PrefetchScalarGridSpec(
        num_scalar_prefetch=4,
        grid=(B, steps),
        in_specs=[qspec, kspec, kspec,
                  pl.BlockSpec((4, DF_DIM), const), pl.BlockSpec((1, LANES), const)],
        out_specs=pl.BlockSpec((tq, DF_W), lambda b, s, qi, kb, f, l: (b * n_q + qi[s], 0)),
        scratch_shapes=[pltpu.VMEM((2 * DF_HEADS, tq, LANES), F32),
                        pltpu.VMEM((2 * DF_HEADS, tq, LANES), F32),
                        pltpu.VMEM((2 * DF_HEADS, tq, LANES), F32)],
    )
    return pl.pallas_call(
        functools.partial(_df_kernel, tq=tq, tk=tk, pos0=pos0, lam_init=lam_init),
        grid_spec=grid_spec,
        out_shape=jax.ShapeDtypeStruct((B * S, DF_W), BF16),
        compiler_params=_params(("parallel", "arbitrary")),
        name="differential_attention",
    )(*sched, q16, k16, v16, lam_vecs, subln_g.reshape(1, LANES))


def _merge_kernel(x_ref, oa_ref, ob_ref, sga_ref, sgb_ref, wa_ref, wb_ref, wo_ref, g_ref, b_ref, o_ref):
    ya = _dot(oa_ref[...], wa_ref[...])
    yb = _dot(ob_ref[...], wb_ref[...])
    mix = (sga_ref[...] * ya + sgb_ref[...] * yb).astype(BF16)
    h = DN_ALPHA * x_ref[...] + _dot(mix, wo_ref[...])
    o_ref[...] = _layer_norm(h, g_ref[...], b_ref[...])


def _merge(xf, oa, ob, sga, sgb, wa16, wb16, wo16, g, b):
    N, D = xf.shape
    tm = 256
    row = lambda i: (i, 0)
    const = lambda i: (0, 0)
    return pl.pallas_call(
        _merge_kernel,
        grid=(N // tm,),
        in_specs=[pl.BlockSpec((tm, D), row), pl.BlockSpec((tm, SB_W), row), pl.BlockSpec((tm, DF_W), row),
                  pl.BlockSpec((tm, D), row), pl.BlockSpec((tm, D), row),
                  pl.BlockSpec((SB_W, D), const), pl.BlockSpec((DF_W, D), const), pl.BlockSpec((D, D), const),
                  pl.BlockSpec((1, D), const), pl.BlockSpec((1, D), const)],
        out_specs=pl.BlockSpec((tm, D), row),
        out_shape=jax.ShapeDtypeStruct((N, D), F32),
        compiler_params=_params(("parallel",)),
        name="merge_ln1",
    )(xf, oa, ob, sga, sgb, wa16, wb16, wo16, g.reshape(1, D), b.reshape(1, D))


def _proj_kernel(x_ref, w_ref, o_ref):
    o_ref[...] = _dot(x_ref[...].astype(BF16), w_ref[...])


def _project(xf, w16):
    N, D = xf.shape
    tm = 256
    return pl.pallas_call(
        _proj_kernel,
        grid=(N // tm,),
        in_specs=[pl.BlockSpec((tm, D), lambda i: (i, 0)), pl.BlockSpec(w16.shape, lambda i: (0, 0))],
        out_specs=pl.BlockSpec((tm, w16.shape[1]), lambda i: (i, 0)),
        out_shape=jax.ShapeDtypeStruct((N, w16.shape[1]), F32),
        compiler_params=_params(("parallel",)),
        name="memory_projection",
    )(xf, w16)


def _memattn_kernel(x_ref, mk_ref, mv_ref, wq_ref, wo_ref, g_ref, b_ref, o_ref):
    x = x_ref[...]
    q = (_dot(x.astype(BF16), wq_ref[...]) * (MEM_DIM ** -0.5)).astype(BF16)
    outs = []
    for h in range(MEM_HEADS):
        sl = slice(h * MEM_DIM, (h + 1) * MEM_DIM)
        sc = _dot_nt(q[:, sl], mk_ref[:, sl].astype(BF16))
        sc = sc - jnp.max(sc, axis=-1, keepdims=True)
        p = jnp.exp(sc)
        p = p / jnp.sum(p, axis=-1, keepdims=True)
        outs.append(_dot(p.astype(BF16), mv_ref[:, sl].astype(BF16)).astype(BF16))
    o = jnp.concatenate(outs, axis=1)
    h2 = DN_ALPHA * x + _dot(o, wo_ref[...])
    o_ref[...] = _layer_norm(h2, g_ref[...], b_ref[...])


def _memory_attention(x1, mem_k, mem_v, wq16, wo16, g, b, *, batch, seq):
    N, D = x1.shape
    M = mem_k.shape[1]
    tm = min(256, seq)
    spb = seq // tm
    const = lambda bi, i: (0, 0)
    return pl.pallas_call(
        _memattn_kernel,
        grid=(batch, spb),
        in_specs=[pl.BlockSpec((tm, D), lambda bi, i: (bi * spb + i, 0)),
                  pl.BlockSpec((None, M, D), lambda bi, i: (bi, 0, 0)),
                  pl.BlockSpec((None, M, D), lambda bi, i: (bi, 0, 0)),
                  pl.BlockSpec((D, D), const), pl.BlockSpec((D, D), const),
                  pl.BlockSpec((1, D), const), pl.BlockSpec((1, D), const)],
        out_specs=pl.BlockSpec((tm, D), lambda bi, i: (bi * spb + i, 0)),
        out_shape=jax.ShapeDtypeStruct((N, D), F32),
        compiler_params=_params(("parallel", "arbitrary")),
        name="memory_attention_ln2",
    )(x1, mem_k, mem_v, wq16, wo16, g.reshape(1, D), b.reshape(1, D))


def _router_kernel(x_ref, w_ref, b_ref, o_ref):
    x = x_ref[...]
    tm = x.shape[0]
    x_hi = x.astype(BF16)
    x_lo = (x - x_hi.astype(F32)).astype(BF16)
    w = w_ref[...]
    w_hi = w.astype(BF16)
    w_lo = (w - w_hi.astype(F32)).astype(BF16)
    logits = _dot(x_hi, w_hi) + _dot(x_lo, w_hi) + _dot(x_hi, w_lo) + b_ref[...]
    lane = lax.broadcasted_iota(jnp.int32, (tm, LANES), 1)
    big = jnp.int32(LANES)
    glog = jnp.where(lane < N_GROUPS, logits, -jnp.inf)
    gmax = jnp.max(glog, axis=-1, keepdims=True)
    grp = jnp.min(jnp.where(glog == gmax, lane, big), axis=-1, keepdims=True)
    g_prob = 1.0 / jnp.sum(jnp.exp(glog - gmax), axis=-1, keepdims=True)
    lo = N_GROUPS + grp * EXPERTS_PER_GROUP
    in_grp = (lane >= lo) & (lane < lo + EXPERTS_PER_GROUP)
    el = jnp.where(in_grp, logits, -jnp.inf)
    v1 = jnp.max(el, axis=-1, keepdims=True)
    i1 = jnp.min(jnp.where(el == v1, lane, big), axis=-1, keepdims=True)
    el2 = jnp.where(lane == i1, -jnp.inf, el)
    v2 = jnp.max(el2, axis=-1, keepdims=True)
    i2 = jnp.min(jnp.where(el2 == v2, lane, big), axis=-1, keepdims=True)
    e2 = jnp.exp(v2 - v1)
    den = 1.0 / (1.0 + e2)
    g1 = den * g_prob
    g2 = e2 * den * g_prob
    out = jnp.where(lane == 0, (i1 - N_GROUPS).astype(F32), 0.0)
    out = jnp.where(lane == 1, (i2 - N_GROUPS).astype(F32), out)
    out = jnp.where(lane == 2, g1, out)
    out = jnp.where(lane == 3, g2, out)
    o_ref[...] = out


def _router(x2, w_group, b_group, w_router, b_router):
    N, D = x2.shape
    tm = 256
    w = jnp.zeros((D, LANES), F32).at[:, :N_GROUPS].set(w_group).at[:, N_GROUPS:N_GROUPS + N_EXPERTS].set(w_router)
    b = jnp.zeros((1, LANES), F32).at[0, :N_GROUPS].set(b_group).at[0, N_GROUPS:N_GROUPS + N_EXPERTS].set(b_router)
    r = pl.pallas_call(
        _router_kernel,
        grid=(N // tm,),
        in_specs=[pl.BlockSpec((tm, D), lambda i: (i, 0)), pl.BlockSpec((D, LANES), lambda i: (0, 0)),
                  pl.BlockSpec((1, LANES), lambda i: (0, 0))],
        out_specs=pl.BlockSpec((tm, LANES), lambda i: (i, 0)),
        out_shape=jax.ShapeDtypeStruct((N, LANES), F32),
        compiler_params=_params(("parallel",)),
        name="router",
    )(x2, w, b)
    expert = r[:, 0:2].astype(jnp.int32)
    gate = r[:, 2:4]
    return expert, gate


def _gather_kernel(tok_ref, x_hbm, o_hbm, sem, *, rows):
    i = pl.program_id(0)

    def copy(r):
        t = tok_ref[0, 0, r]
        return pltpu.make_async_copy(x_hbm.at[pl.ds(t, 1)], o_hbm.at[pl.ds(i * rows + r, 1)], sem)

    def start(r, c):
        copy(r).start()
        return c

    def wait(r, c):
        copy(r).wait()
        return c

    lax.fori_loop(0, rows, start, 0)
    lax.fori_loop(0, rows, wait, 0)


def _gather_rows(x2, tok):
    N, D = x2.shape
    R = tok.shape[0]
    rows = EBLOCK
    nb = R // rows
    return pl.pallas_call(
        functools.partial(_gather_kernel, rows=rows),
        grid=(nb,),
        in_specs=[pl.BlockSpec((1, 1, rows), lambda i: (i, 0, 0), memory_space=pltpu.SMEM),
                  pl.BlockSpec(memory_space=pl.ANY)],
        out_specs=pl.BlockSpec(memory_space=pl.ANY),
        out_shape=jax.ShapeDtypeStruct((R, D), F32),
        scratch_shapes=[pltpu.SemaphoreType.DMA(())],
        compiler_params=_params(("arbitrary",)),
        name="expert_gather",
    )(tok.reshape(nb, 1, rows), x2)


def _expert_kernel(be_ref, x_ref, wg_ref, wu_ref, wd_ref, o_ref):
    xb = x_ref[...].astype(BF16)
    hg = _dot(xb, wg_ref[...])
    hu = _dot(xb, wu_ref[...])
    hid = (hg * jax.nn.sigmoid(hg) * hu).astype(BF16)
    o_ref[...] = _dot(hid, wd_ref[...])


def _experts(xs, blk_e, wg16, wu16, wd16):
    R, D = xs.shape
    nb = R // EBLOCK
    grid_spec = pltpu.PrefetchScalarGridSpec(
        num_scalar_prefetch=1,
        grid=(nb,),
        in_specs=[pl.BlockSpec((EBLOCK, D), lambda i, be: (i, 0)),
                  pl.BlockSpec((None, D, D_EXPERT), lambda i, be: (be[i], 0, 0)),
                  pl.BlockSpec((None, D, D_EXPERT), lambda i, be: (be[i], 0, 0)),
                  pl.BlockSpec((None, D_EXPERT, D), lambda i, be: (be[i], 0, 0))],
        out_specs=pl.BlockSpec((EBLOCK, D), lambda i, be: (i, 0)),
    )
    return pl.pallas_call(
        _expert_kernel,
        grid_spec=grid_spec,
        out_shape=jax.ShapeDtypeStruct((R, D), F32),
        compiler_params=_params(("arbitrary",)),
        name="grouped_experts",
    )(blk_e, xs, wg16, wu16, wd16)


def _combine_kernel(pos_ref, x_ref, gate_ref, g_ref, b_ref, y_hbm, o_ref, buf_ref, sem, *, tm):
    def copy(r, k):
        p = pos_ref[0, k, r]
        return pltpu.make_async_copy(y_hbm.at[pl.ds(p, 1)], buf_ref.at[k, pl.ds(r, 1)], sem)

    def start(r, c):
        copy(r, 0).start()
        copy(r, 1).start()
        return c

    def wait(r, c):
        copy(r, 0).wait()
        copy(r, 1).wait()
        return c

    lax.fori_loop(0, tm, start, 0)
    lax.fori_loop(0, tm, wait, 0)
    gate = gate_ref[...]
    moe = buf_ref[0] * gate[:, 0:1] + buf_ref[1] * gate[:, 1:2]
    h = DN_ALPHA * x_ref[...] + moe
    o_ref[...] = _layer_norm(h, g_ref[...], b_ref[...])


def _combine(x2, ys, pos, gate, g, b):
    N, D = x2.shape
    tm = 256
    nt = N // tm
    pos3 = pos.reshape(nt, tm, TOP_K).transpose(0, 2, 1)
    gate_p = jnp.zeros((N, LANES), F32).at[:, 0:TOP_K].set(gate)
    return pl.pallas_call(
        functools.partial(_combine_kernel, tm=tm),
        grid=(nt,),
        in_specs=[pl.BlockSpec((1, TOP_K, tm), lambda i: (i, 0, 0), memory_space=pltpu.SMEM),
                  pl.BlockSpec((tm, D), lambda i: (i, 0)),
                  pl.BlockSpec((tm, LANES), lambda i: (i, 0)),
                  pl.BlockSpec((1, D), lambda i: (0, 0)), pl.BlockSpec((1, D), lambda i: (0, 0)),
                  pl.BlockSpec(memory_space=pl.ANY)],
        out_specs=pl.BlockSpec((tm, D), lambda i: (i, 0)),
        out_shape=jax.ShapeDtypeStruct((N, D), F32),
        scratch_shapes=[pltpu.VMEM((TOP_K, tm, D), F32), pltpu.SemaphoreType.DMA(())],
        compiler_params=_params(("arbitrary",)),
        name="combine_ln3",
    )(pos3, x2, gate_p, g.reshape(1, D), b.reshape(1, D), ys)


def _dispatch_plan(expert):
    N, K = expert.shape
    A = N * K
    E = N_EXPERTS
    flat_e = expert.reshape(-1)
    order = jnp.argsort(flat_e)
    sorted_e = flat_e[order]
    counts = jnp.bincount(flat_e, length=E)
    padded = (counts + EBLOCK - 1) // EBLOCK * EBLOCK
    pad_end = jnp.cumsum(padded)
    pad_start = pad_end - padded
    start = jnp.cumsum(counts) - counts
    dest = (pad_start[sorted_e] + jnp.arange(A, dtype=jnp.int32) - start[sorted_e]).astype(jnp.int32)
    n_blocks = (A + E * (EBLOCK - 1) + EBLOCK - 1) // EBLOCK
    rows = n_blocks * EBLOCK
    tok = jnp.zeros((rows,), jnp.int32).at[dest].set((order // K).astype(jnp.int32))
    pos = jnp.zeros((A,), jnp.int32).at[order].set(dest).reshape(N, K)
    blk_e = jnp.minimum(jnp.searchsorted(pad_end, jnp.arange(n_blocks) * EBLOCK, side='right'), E - 1)
    return tok, pos, blk_e.astype(jnp.int32)


def _hier_moe_ln3(x2, w_group, b_group, w_router, b_router, wg16, wu16, wd16, g, b):
    expert, gate = _router(x2, w_group, b_group, w_router, b_router)
    tok, pos, blk_e = _dispatch_plan(expert)
    xs = _gather_rows(x2, tok)
    ys = _experts(xs, blk_e, wg16, wu16, wd16)
    return _combine(x2, ys, pos, gate, g, b)


def _layer(x, caches, mem_k, mem_v, lam_init, w):
    B, S, D = x.shape
    N = B * S
    xf = x.reshape(N, D)
    past = 0 if caches is None else caches[0].shape[1]
    (ka, va, kb, vb, sga, sgb, qa16, ka16, va16, qb16, kb16, vb16) = _in_projection(x, w['w_in'], past)
    if caches is None:
        tq = min(512, S)
        tk = min(256, S)
        keys = (ka16, va16, kb16, vb16)
    else:
        tq = S
        tk = 256
        keys = tuple(_pack_keys(c.reshape(B, past, -1), n, tk)
                     for c, n in zip(caches, (ka16, va16, kb16, vb16)))
    oa = _sb_attention(qa16, keys[0], keys[1], tq=tq, tk=tk, pos0=past)
    ob = _df_attention(qb16, keys[2], keys[3], w['lam'], w['subln_g'], tq=tq, tk=tk, pos0=past,
                       lam_init=lam_init)
    x1 = _merge(xf, oa, ob, sga, sgb, w['w_branch_a'], w['w_branch_b'], w['w_out'], w['ln1_g'], w['ln1_b'])
    x2 = _memory_attention(x1, mem_k, mem_v, w['w_mq'], w['w_mo'], w['ln2_g'], w['ln2_b'], batch=B, seq=S)
    x3 = _hier_moe_ln3(x2, w['w_group'], w['b_group'], w['w_router'], w['b_router'],
                       w['w_gate'], w['w_up'], w['w_down'], w['ln3_g'], w['ln3_b'])
    return (x3.reshape(B, S, D), ka.reshape(B, S, SB_HEADS, SB_DIM), va.reshape(B, S, SB_HEADS, SB_DIM),
            kb.reshape(B, S, DF_HEADS, 2 * DF_DIM), vb.reshape(B, S, DF_HEADS, 2 * DF_DIM))


def kernel(x_prompt, mem_prompt, x_sample, cache_sb_k, cache_sb_v, cache_diff_k, cache_diff_v, cache_mem_k, cache_mem_v, w_in, lam_q1, lam_k1, lam_q2, lam_k2, subln_g, w_branch_a, w_branch_b, w_out, ln1_g, ln1_b, w_mq, w_mk, w_mv, w_mo, ln2_g, ln2_b, w_group, b_group, w_router, b_router, w_up, w_gate, w_down, ln3_g, ln3_b):
    depth = w_in.shape[0]
    yp, ys = x_prompt, x_sample
    Bp, n_mem = mem_prompt.shape[0], mem_prompt.shape[1]
    Bs = x_sample.shape[0]
    outs = [[] for _ in range(10)]
    for l in range(depth):
        lam_init = 0.8 - 0.6 * math.exp(-0.3 * l)
        w = {
            'w_in': w_in[l].astype(BF16),
            'lam': jnp.stack([lam_q1[l], lam_k1[l], lam_q2[l], lam_k2[l]]).astype(F32),
            'subln_g': subln_g[l],
            'w_branch_a': w_branch_a[l].astype(BF16), 'w_branch_b': w_branch_b[l].astype(BF16),
            'w_out': w_out[l].astype(BF16), 'ln1_g': ln1_g[l], 'ln1_b': ln1_b[l],
            'w_mq': w_mq[l].astype(BF16), 'w_mo': w_mo[l].astype(BF16),
            'ln2_g': ln2_g[l], 'ln2_b': ln2_b[l],
            'w_group': w_group[l], 'b_group': b_group[l], 'w_router': w_router[l], 'b_router': b_router[l],
            'w_gate': w_gate[l].astype(BF16), 'w_up': w_up[l].astype(BF16), 'w_down': w_down[l].astype(BF16),
            'ln3_g': ln3_g[l], 'ln3_b': ln3_b[l],
        }
        memf = mem_prompt.reshape(Bp * n_mem, D_MODEL)
        mk = _project(memf, w_mk[l].astype(BF16)).reshape(Bp, n_mem, D_MODEL)
        mv = _project(memf, w_mv[l].astype(BF16)).reshape(Bp, n_mem, D_MODEL)
        yp, ka, va, kb, vb = _layer(yp, None, mk, mv, lam_init, w)
        for lst, val in zip(outs[0:6], (ka, va, kb, vb,
                                        mk.reshape(Bp, n_mem, MEM_HEADS, MEM_DIM),
                                        mv.reshape(Bp, n_mem, MEM_HEADS, MEM_DIM))):
            lst.append(val)
        caches = (cache_sb_k[l], cache_sb_v[l], cache_diff_k[l], cache_diff_v[l])
        ys, ka_s, va_s, kb_s, vb_s = _layer(ys, caches,
                                            cache_mem_k[l].reshape(Bs, n_mem, D_MODEL),
                                            cache_mem_v[l].reshape(Bs, n_mem, D_MODEL), lam_init, w)
        for lst, val in zip(outs[6:10], (ka_s, va_s, kb_s, vb_s)):
            lst.append(val)
    return (yp, ys) + tuple(jnp.stack(o) for o in outs)
```

```python
import functools
import math

import jax
import jax.numpy as jnp
import numpy as np
from jax import lax
from jax.experimental import pallas as pl
from jax.experimental.pallas import tpu as pltpu

F32 = jnp.float32
BF16 = jnp.bfloat16

D_MODEL = 1024
SB_HEADS = 8
SB_DIM = 64
SB_W = SB_HEADS * SB_DIM
DF_HEADS = 4
DF_DIM = 64
DF_W = DF_HEADS * 2 * DF_DIM
CHUNK = 64
MEM_HEADS = 4
MEM_DIM = D_MODEL // MEM_HEADS
N_GROUPS = 4
EXPERTS_PER_GROUP = 8
N_EXPERTS = N_GROUPS * EXPERTS_PER_GROUP
TOP_K = 2
D_EXPERT = 512
EBLOCK = 256
ROPE_THETA = 10000.0
LN_EPS = 1e-5
RMS_EPS = 1e-5
DEPTH = 1
DN_ALPHA = (2 * DEPTH) ** 0.25
IN_W = 3 * SB_W + 3 * DF_W + 2 * D_MODEL

LANES = 128
HEAD_GROUPS = 4
NEG_BIG = -1e30
VMEM_LIMIT = 48 * 1024 * 1024


def _dot(a, b):
    return jnp.dot(a, b, preferred_element_type=F32)


def _dot_nt(a, b):
    return lax.dot_general(a, b, (((1,), (1,)), ((), ())), preferred_element_type=F32)


def _layer_norm(h, g, b):
    mu = jnp.mean(h, axis=-1, keepdims=True)
    d = h - mu
    var = jnp.mean(d * d, axis=-1, keepdims=True)
    return d * lax.rsqrt(var + LN_EPS) * g + b


def _params(sem, vmem=VMEM_LIMIT):
    return pltpu.CompilerParams(dimension_semantics=sem, vmem_limit_bytes=vmem)


def _inproj_kernel(x_ref, w_ref, cos_ref, sina_ref, sinb_ref,
                   ka_ref, va_ref, kb_ref, vb_ref, sga_ref, sgb_ref,
                   qa16_ref, ka16_ref, va16_ref, qb16_ref, kb16_ref, vb16_ref):
    xb = x_ref[...].astype(BF16)

    def mm(off, width):
        return _dot(xb, w_ref[:, off:off + width])

    def put_groups(ref, val):
        for g in range(HEAD_GROUPS):
            ref[g] = val[:, g * LANES:(g + 1) * LANES].astype(BF16)

    def rope(t):
        n = t.shape[1]
        reps = n // LANES
        cos = jnp.concatenate([cos_ref[...]] * reps, axis=1)
        sina = jnp.concatenate([sina_ref[...]] * reps, axis=1)
        sinb = jnp.concatenate([sinb_ref[...]] * reps, axis=1)
        fwd = pltpu.roll(t, n - DF_DIM // 2, 1)
        bwd = pltpu.roll(t, DF_DIM // 2, 1)
        return t * cos + fwd * sina + bwd * sinb

    qa = mm(0, SB_W)
    put_groups(qa16_ref, qa * (SB_DIM ** -0.5))
    ka = mm(SB_W, SB_W)
    ka_ref[...] = ka
    put_groups(ka16_ref, ka)
    va = mm(2 * SB_W, SB_W)
    va_ref[...] = va
    put_groups(va16_ref, va)
    off = 3 * SB_W
    qb = rope(mm(off, DF_W))
    put_groups(qb16_ref, qb * (DF_DIM ** -0.5))
    kb = rope(mm(off + DF_W, DF_W))
    kb_ref[...] = kb
    put_groups(kb16_ref, kb)
    vb = mm(off + 2 * DF_W, DF_W)
    vb_ref[...] = vb
    put_groups(vb16_ref, vb)
    off = off + 3 * DF_W
    sga_ref[...] = jax.nn.sigmoid(mm(off, D_MODEL))
    sgb_ref[...] = jax.nn.sigmoid(mm(off + D_MODEL, D_MODEL))


def _rope_tables(pos0, seq):
    half = DF_DIM // 2
    inv_freq = jnp.exp(jnp.arange(half, dtype=F32) * (-math.log(ROPE_THETA) / half))
    ang = (pos0 + jnp.arange(seq, dtype=jnp.int32)).astype(F32)[:, None] * inv_freq[None, :]
    cos, sin = jnp.cos(ang), jnp.sin(ang)
    zero = jnp.zeros_like(sin)
    reps = LANES // DF_DIM
    cos_t = jnp.tile(jnp.concatenate([cos, cos], axis=1), (1, reps))
    sina_t = jnp.tile(jnp.concatenate([-sin, zero], axis=1), (1, reps))
    sinb_t = jnp.tile(jnp.concatenate([zero, sin], axis=1), (1, reps))
    return cos_t, sina_t, sinb_t


def _in_projection(x, w_in16, pos0):
    B, S, D = x.shape
    N = B * S
    tm = min(256, S)
    spb = S // tm
    xf = x.reshape(N, D)
    cos_t, sina_t, sinb_t = _rope_tables(pos0, S)
    row = lambda i: (i, 0)
    tab = lambda i: (i % spb, 0)
    grp = lambda i: (i // spb, 0, i % spb, 0)
    f32_512 = jax.ShapeDtypeStruct((N, SB_W), F32)
    f32_1024 = jax.ShapeDtypeStruct((N, D), F32)
    g16 = jax.ShapeDtypeStruct((B, HEAD_GROUPS, S, LANES), BF16)
    spec512 = pl.BlockSpec((tm, SB_W), row)
    spec1024 = pl.BlockSpec((tm, D), row)
    gspec = pl.BlockSpec((None, HEAD_GROUPS, tm, LANES), grp)
    tspec = pl.BlockSpec((tm, LANES), tab)
    return pl.pallas_call(
        _inproj_kernel,
        grid=(N // tm,),
        in_specs=[spec1024, pl.BlockSpec((D, IN_W), lambda i: (0, 0)), tspec, tspec, tspec],
        out_specs=[spec512, spec512, spec512, spec512, spec1024, spec1024,
                   gspec, gspec, gspec, gspec, gspec, gspec],
        out_shape=[f32_512, f32_512, f32_512, f32_512, f32_1024, f32_1024,
                   g16, g16, g16, g16, g16, g16],
        compiler_params=_params(("parallel",), 56 * 1024 * 1024),
        name="in_projection",
    )(xf, w_in16, cos_t, sina_t, sinb_t)


def _pack_kernel(c_ref, n_ref, o_ref, *, n_cache_blocks, new_rows):
    j = pl.program_id(1)

    @pl.when(j < n_cache_blocks)
    def _():
        c = c_ref[...]
        for g in range(HEAD_GROUPS):
            o_ref[g] = c[:, g * LANES:(g + 1) * LANES].astype(BF16)

    @pl.when(j == n_cache_blocks)
    def _():
        o_ref[...] = jnp.zeros(o_ref.shape, BF16)
        o_ref[:, 0:new_rows, :] = n_ref[...]


def _pack_keys(cache, new16, tk):
    B, P, W = cache.shape
    S = new16.shape[2]
    ncb = P // tk
    return pl.pallas_call(
        functools.partial(_pack_kernel, n_cache_blocks=ncb, new_rows=S),
        grid=(B, ncb + 1),
        in_specs=[pl.BlockSpec((None, tk, W), lambda b, j: (b, jnp.minimum(j, ncb - 1), 0)),
                  pl.BlockSpec((None, HEAD_GROUPS, S, LANES), lambda b, j: (b, 0, 0, 0))],
        out_specs=pl.BlockSpec((None, HEAD_GROUPS, tk, LANES), lambda b, j: (b, 0, j, 0)),
        out_shape=jax.ShapeDtypeStruct((B, HEAD_GROUPS, P + tk, LANES), BF16),
        compiler_params=_params(("parallel", "arbitrary")),
        name="pack_keys",
    )(cache, new16)


def _schedule(n_q, tq, tk, pos0, fully_visible):
    qi, kb, first, last, masked = [], [], [], [], []
    for i in range(n_q):
        top = -(-(pos0 + (i + 1) * tq) // tk) - 1
        for j in range(top, -1, -1):
            qi.append(i)
            kb.append(j)
            first.append(1 if j == top else 0)
            last.append(1 if j == 0 else 0)
            masked.append(0 if fully_visible(pos0 + i * tq, (j + 1) * tk - 1) else 1)
    mk = lambda v: jnp.asarray(np.asarray(v, np.int32))
    return mk(qi), mk(kb), mk(first), mk(last), mk(masked)


def _half_masks():
    lane = lax.broadcasted_iota(jnp.int32, (1, LANES), 1)
    lo = jnp.where(lane < LANES // 2, 1.0, 0.0).astype(BF16)
    hi = jnp.where(lane >= LANES // 2, 1.0, 0.0).astype(BF16)
    return lo, hi


def _positions(qi, kb, tq, tk, pos0):
    q_pos = pos0 + qi * tq + lax.broadcasted_iota(jnp.int32, (tq, 1), 0)
    k_pos = kb * tk + lax.broadcasted_iota(jnp.int32, (1, tk), 1)
    return q_pos, k_pos


LOG2E = 1.4426950408889634
SIGN_BIT = 0x80000000
BF16_BITS = 0xFFFF0000


def _sb_kernel(qi_ref, kb_ref, first_ref, last_ref, masked_ref, q_ref, k_ref, v_ref, tri_ref, o_ref,
               acc_ref, carry_ref, *, tq, tk, pos0):
    s = pl.program_id(1)
    qi = qi_ref[s]
    kb = kb_ref[s]

    @pl.when(first_ref[s] == 1)
    def _():
        acc_ref[...] = jnp.zeros(acc_ref.shape, F32)
        carry_ref[...] = jnp.zeros(carry_ref.shape, F32)

    reps = tk // LANES

    def tile(use_mask):
        half = _half_masks()
        tri2 = tri_ref[...]
        if use_mask:
            q_pos, k_pos = _positions(qi, kb, tq, tk, pos0)
            mask = k_pos < q_pos

        def group(g, c):
            q = q_ref[g]
            k = k_ref[g]
            v = v_ref[g]
            upd = jnp.zeros((tq, LANES), F32)
            for j in range(2):
                y = _dot_nt(q * half[j], k) * LOG2E
                neg_abs = pltpu.bitcast(pltpu.bitcast(y, jnp.uint32) | jnp.uint32(SIGN_BIT), F32)
                sp = jnp.maximum(y, 0.0) + jnp.log2(1.0 + jnp.exp2(neg_abs))
                if use_mask:
                    sp = jnp.where(mask, sp, 0.0)
                hi = pltpu.bitcast(pltpu.bitcast(sp, jnp.uint32) & jnp.uint32(BF16_BITS), F32)
                lo = sp - hi
                split = jnp.concatenate([hi.astype(BF16), lo.astype(BF16)], axis=1)
                suffix = _dot(split, tri2)
                carry = carry_ref[2 * g + j]
                total = suffix + jnp.concatenate([carry] * reps, axis=1)
                a = jnp.exp2(y - total)
                if use_mask:
                    a = jnp.where(mask, a, 0.0)
                upd = upd + _dot(a.astype(BF16), v * half[j])
                carry_ref[2 * g + j] = carry + jnp.broadcast_to(suffix[:, 0:1], (tq, LANES))
            acc_ref[g] = acc_ref[g] + upd
            return c

        lax.fori_loop(0, HEAD_GROUPS, group, 0, unroll=True)

    @pl.when(masked_ref[s] == 1)
    def _():
        tile(True)

    @pl.when(masked_ref[s] == 0)
    def _():
        tile(False)

    @pl.when(last_ref[s] == 1)
    def _():
        for g in range(HEAD_GROUPS):
            o_ref[:, g * LANES:(g + 1) * LANES] = acc_ref[g].astype(BF16)


def _tri_ones_stacked(tk):
    r = np.arange(tk)
    tri = (r[:, None] >= r[None, :]).astype(np.float32)
    return jnp.asarray(np.concatenate([tri, tri], axis=0)).astype(BF16)


def _sb_attention(q16, k16, v16, *, tq, tk, pos0):
    B, _, S, _ = q16.shape
    n_q = S // tq
    sched = _schedule(n_q, tq, tk, pos0, lambda q_first, k_last: k_last < q_first)
    steps = int(sched[0].shape[0])
    qspec = pl.BlockSpec((None, HEAD_GROUPS, tq, LANES), lambda b, s, qi, kb, f, l, m: (b, 0, qi[s], 0))
    kspec = pl.BlockSpec((None, HEAD_GROUPS, tk, LANES), lambda b, s, qi, kb, f, l, m: (b, 0, kb[s], 0))
    grid_spec = pltpu.PrefetchScalarGridSpec(
        num_scalar_prefetch=5,
        grid=(B, steps),
        in_specs=[qspec, kspec, kspec,
                  pl.BlockSpec((2 * tk, tk), lambda b, s, qi, kb, f, l, m: (0, 0))],
        out_specs=pl.BlockSpec((tq, SB_W), lambda b, s, qi, kb, f, l, m: (b * n_q + qi[s], 0)),
        scratch_shapes=[pltpu.VMEM((HEAD_GROUPS, tq, LANES), F32),
                        pltpu.VMEM((SB_HEADS, tq, LANES), F32)],
    )
    return pl.pallas_call(
        functools.partial(_sb_kernel, tq=tq, tk=tk, pos0=pos0),
        grid_spec=grid_spec,
        out_shape=jax.ShapeDtypeStruct((B * S, SB_W), BF16),
        compiler_params=_params(("parallel", "arbitrary")),
        name="stick_breaking_attention",
    )(*sched, q16, k16, v16, _tri_ones_stacked(tk))


def _df_kernel(qi_ref, kb_ref, first_ref, last_ref, masked_ref, q_ref, k_ref, v_ref, lam_ref, g_ref,
               o_ref, m_ref, l_ref, acc_ref, *, tq, tk, pos0, lam_init):
    s = pl.program_id(1)
    qi = qi_ref[s]
    kb = kb_ref[s]

    @pl.when(first_ref[s] == 1)
    def _():
        m_ref[...] = jnp.full(m_ref.shape, NEG_BIG, F32)
        l_ref[...] = jnp.zeros(l_ref.shape, F32)
        acc_ref[...] = jnp.zeros(acc_ref.shape, F32)

    reps = tk // LANES

    def tile(use_mask):
        half = _half_masks()
        if use_mask:
            q_pos, k_pos = _positions(qi, kb, tq, tk, pos0)
            mask = k_pos <= (q_pos | (CHUNK - 1))

        def head(h, c):
            q = q_ref[h]
            k = k_ref[h]
            v = v_ref[h]
            for j in range(2):
                idx = 2 * h + j
                sc = _dot_nt(q * half[j], k)
                if use_mask:
                    sc = jnp.where(mask, sc, NEG_BIG)
                m_old = m_ref[idx]
                m_new = jnp.maximum(m_old, jnp.max(sc, axis=1, keepdims=True))
                alpha = jnp.exp(m_old - m_new)
                p = jnp.exp(sc - jnp.concatenate([m_new] * reps, axis=1))
                l_ref[idx] = alpha * l_ref[idx] + jnp.sum(p, axis=1, keepdims=True)
                acc_ref[idx] = alpha * acc_ref[idx] + _dot(p.astype(BF16), v)
                m_ref[idx] = m_new
            return c

        lax.fori_loop(0, DF_HEADS, head, 0, unroll=True)

    @pl.when(masked_ref[s] == 1)
    def _():
        tile(True)

    @pl.when(masked_ref[s] == 0)
    def _():
        tile(False)

    @pl.when(last_ref[s] == 1)
    def _():
        lv = lam_ref[...]
        lam = (jnp.exp(jnp.sum(lv[0:1] * lv[1:2], axis=1, keepdims=True))
               - jnp.exp(jnp.sum(lv[2:3] * lv[3:4], axis=1, keepdims=True)) + lam_init)
        gain = g_ref[...] * (1.0 - lam_init)
        for h in range(DF_HEADS):
            o = acc_ref[2 * h] / l_ref[2 * h] - lam * (acc_ref[2 * h + 1] / l_ref[2 * h + 1])
            o = o * lax.rsqrt(jnp.mean(o * o, axis=-1, keepdims=True) + RMS_EPS)
            o_ref[:, h * LANES:(h + 1) * LANES] = (o * gain).astype(BF16)


def _df_attention(q16, k16, v16, lam_vecs, subln_g, *, tq, tk, pos0, lam_init):
    B, _, S, _ = q16.shape
    n_q = S // tq
    sched = _schedule(n_q, tq, tk, pos0, lambda q_first, k_last: k_last <= (q_first | (CHUNK - 1)))
    steps = int(sched[0].shape[0])
    qspec = pl.BlockSpec((None, DF_HEADS, tq, LANES), lambda b, s, qi, kb, f, l, m: (b, 0, qi[s], 0))
    kspec = pl.BlockSpec((None, DF_HEADS, tk, LANES), lambda b, s, qi, kb, f, l, m: (b, 0, kb[s], 0))
    const = lambda b, s, qi, kb, f, l, m: (0, 0)
    grid_spec = pltpu.PrefetchScalarGridSpec(
        num_scalar_prefetch=5,
        grid=(B, steps),
        in_specs=[qspec, kspec, kspec,
                  pl.BlockSpec((4, DF_DIM), const), pl.BlockSpec((1, LANES), const)],
        out_specs=pl.BlockSpec((tq, DF_W), lambda b, s, qi, kb, f, l, m: (b * n_q + qi[s], 0)),
        scratch_shapes=[pltpu.VMEM((2 * DF_HEADS, tq, LANES), F32),
                        pltpu.VMEM((2 * DF_HEADS, tq, LANES), F32),
                        pltpu.VMEM((2 * DF_HEADS, tq, LANES), F32)],
    )
    return pl.pallas_call(
        functools.partial(_df_kernel, tq=tq, tk=tk, pos0=pos0, lam_init=lam_init),
        grid_spec=grid_spec,
        out_shape=jax.ShapeDtypeStruct((B * S, DF_W), BF16),
        compiler_params=_params(("parallel", "arbitrary")),
        name="differential_attention",
    )(*sched, q16, k16, v16, lam_vecs, subln_g.reshape(1, LANES))


def _merge_kernel(x_ref, oa_ref, ob_ref, sga_ref, sgb_ref, wa_ref, wb_ref, wo_ref, g_ref, b_ref, o_ref):
    ya = _dot(oa_ref[...], wa_ref[...])
    yb = _dot(ob_ref[...], wb_ref[...])
    mix = (sga_ref[...] * ya + sgb_ref[...] * yb).astype(BF16)
    h = DN_ALPHA * x_ref[...] + _dot(mix, wo_ref[...])
    o_ref[...] = _layer_norm(h, g_ref[...], b_ref[...])


def _merge(xf, oa, ob, sga, sgb, wa16, wb16, wo16, g, b):
    N, D = xf.shape
    tm = 256
    row = lambda i: (i, 0)
    const = lambda i: (0, 0)
    return pl.pallas_call(
        _merge_kernel,
        grid=(N // tm,),
        in_specs=[pl.BlockSpec((tm, D), row), pl.BlockSpec((tm, SB_W), row), pl.BlockSpec((tm, DF_W), row),
                  pl.BlockSpec((tm, D), row), pl.BlockSpec((tm, D), row),
                  pl.BlockSpec((SB_W, D), const), pl.BlockSpec((DF_W, D), const), pl.BlockSpec((D, D), const),
                  pl.BlockSpec((1, D), const), pl.BlockSpec((1, D), const)],
        out_specs=pl.BlockSpec((tm, D), row),
        out_shape=jax.ShapeDtypeStruct((N, D), F32),
        compiler_params=_params(("parallel",)),
        name="merge_ln1",
    )(xf, oa, ob, sga, sgb, wa16, wb16, wo16, g.reshape(1, D), b.reshape(1, D))


def _proj_kernel(x_ref, w_ref, o_ref):
    o_ref[...] = _dot(x_ref[...].astype(BF16), w_ref[...])


def _project(xf, w16):
    N, D = xf.shape
    tm = 256
    return pl.pallas_call(
        _proj_kernel,
        grid=(N // tm,),
        in_specs=[pl.BlockSpec((tm, D), lambda i: (i, 0)), pl.BlockSpec(w16.shape, lambda i: (0, 0))],
        out_specs=pl.BlockSpec((tm, w16.shape[1]), lambda i: (i, 0)),
        out_shape=jax.ShapeDtypeStruct((N, w16.shape[1]), F32),
        compiler_params=_params(("parallel",)),
        name="memory_projection",
    )(xf, w16)


def _memattn_kernel(x_ref, mk_ref, mv_ref, wq_ref, wo_ref, g_ref, b_ref, o_ref):
    x = x_ref[...]
    q = (_dot(x.astype(BF16), wq_ref[...]) * (MEM_DIM ** -0.5)).astype(BF16)
    outs = []
    for h in range(MEM_HEADS):
        sl = slice(h * MEM_DIM, (h + 1) * MEM_DIM)
        sc = _dot_nt(q[:, sl], mk_ref[:, sl].astype(BF16))
        sc = sc - jnp.max(sc, axis=-1, keepdims=True)
        p = jnp.exp(sc)
        p = p / jnp.sum(p, axis=-1, keepdims=True)
        outs.append(_dot(p.astype(BF16), mv_ref[:, sl].astype(BF16)).astype(BF16))
    o = jnp.concatenate(outs, axis=1)
    h2 = DN_ALPHA * x + _dot(o, wo_ref[...])
    o_ref[...] = _layer_norm(h2, g_ref[...], b_ref[...])


def _memory_attention(x1, mem_k, mem_v, wq16, wo16, g, b, *, batch, seq):
    N, D = x1.shape
    M = mem_k.shape[1]
    tm = min(256, seq)
    spb = seq // tm
    const = lambda bi, i: (0, 0)
    return pl.pallas_call(
        _memattn_kernel,
        grid=(batch, spb),
        in_specs=[pl.BlockSpec((tm, D), lambda bi, i: (bi * spb + i, 0)),
                  pl.BlockSpec((None, M, D), lambda bi, i: (bi, 0, 0)),
                  pl.BlockSpec((None, M, D), lambda bi, i: (bi, 0, 0)),
                  pl.BlockSpec((D, D), const), pl.BlockSpec((D, D), const),
                  pl.BlockSpec((1, D), const), pl.BlockSpec((1, D), const)],
        out_specs=pl.BlockSpec((tm, D), lambda bi, i: (bi * spb + i, 0)),
        out_shape=jax.ShapeDtypeStruct((N, D), F32),
        compiler_params=_params(("parallel", "arbitrary")),
        name="memory_attention_ln2",
    )(x1, mem_k, mem_v, wq16, wo16, g.reshape(1, D), b.reshape(1, D))


def _router_kernel(x_ref, w_ref, b_ref, o_ref):
    x = x_ref[...]
    tm = x.shape[0]
    x_hi = x.astype(BF16)
    x_lo = (x - x_hi.astype(F32)).astype(BF16)
    w = w_ref[...]
    w_hi = w.astype(BF16)
    w_lo = (w - w_hi.astype(F32)).astype(BF16)
    logits = _dot(x_hi, w_hi) + _dot(x_lo, w_hi) + _dot(x_hi, w_lo) + b_ref[...]
    lane = lax.broadcasted_iota(jnp.int32, (tm, LANES), 1)
    big = jnp.int32(LANES)
    glog = jnp.where(lane < N_GROUPS, logits, -jnp.inf)
    gmax = jnp.max(glog, axis=-1, keepdims=True)
    grp = jnp.min(jnp.where(glog == gmax, lane, big), axis=-1, keepdims=True)
    g_prob = 1.0 / jnp.sum(jnp.exp(glog - gmax), axis=-1, keepdims=True)
    lo = N_GROUPS + grp * EXPERTS_PER_GROUP
    in_grp = (lane >= lo) & (lane < lo + EXPERTS_PER_GROUP)
    el = jnp.where(in_grp, logits, -jnp.inf)
    v1 = jnp.max(el, axis=-1, keepdims=True)
    i1 = jnp.min(jnp.where(el == v1, lane, big), axis=-1, keepdims=True)
    el2 = jnp.where(lane == i1, -jnp.inf, el)
    v2 = jnp.max(el2, axis=-1, keepdims=True)
    i2 = jnp.min(jnp.where(el2 == v2, lane, big), axis=-1, keepdims=True)
    e2 = jnp.exp(v2 - v1)
    den = 1.0 / (1.0 + e2)
    g1 = den * g_prob
    g2 = e2 * den * g_prob
    out = jnp.where(lane == 0, (i1 - N_GROUPS).astype(F32), 0.0)
    out = jnp.where(lane == 1, (i2 - N_GROUPS).astype(F32), out)
    out = jnp.where(lane == 2, g1, out)
    out = jnp.where(lane == 3, g2, out)
    o_ref[...] = out


def _router(x2, w_group, b_group, w_router, b_router):
    N, D = x2.shape
    tm = 256
    w = jnp.zeros((D, LANES), F32).at[:, :N_GROUPS].set(w_group).at[:, N_GROUPS:N_GROUPS + N_EXPERTS].set(w_router)
    b = jnp.zeros((1, LANES), F32).at[0, :N_GROUPS].set(b_group).at[0, N_GROUPS:N_GROUPS + N_EXPERTS].set(b_router)
    r = pl.pallas_call(
        _router_kernel,
        grid=(N // tm,),
        in_specs=[pl.BlockSpec((tm, D), lambda i: (i, 0)), pl.BlockSpec((D, LANES), lambda i: (0, 0)),
                  pl.BlockSpec((1, LANES), lambda i: (0, 0))],
        out_specs=pl.BlockSpec((tm, LANES), lambda i: (i, 0)),
        out_shape=jax.ShapeDtypeStruct((N, LANES), F32),
        compiler_params=_params(("parallel",)),
        name="router",
    )(x2, w, b)
    expert = r[:, 0:2].astype(jnp.int32)
    gate = r[:, 2:4]
    return expert, gate


def _row_gather(idx_ref, k, src_hbm, dst_ref, sem, rows):
    def body(r, c):
        t = idx_ref[0, k, r]
        pltpu.make_async_copy(src_hbm.at[pl.ds(t, 1)], dst_ref.at[pl.ds(r, 1)], sem).start()
        return c
    lax.fori_loop(0, rows, body, 0, unroll=8)


def _gather_wait(src_hbm, dst_ref, sem, rows):
    pltpu.make_async_copy(src_hbm.at[pl.ds(0, rows)], dst_ref, sem).wait()


def _expert_kernel(be_ref, tok_cur_ref, tok_next_ref, x_hbm, wg_ref, wu_ref, wd_ref, o_ref,
                   buf_ref, sem_ref, *, n_blocks):
    i = pl.program_id(0)
    slot = i % 2

    @pl.when(i == 0)
    def _():
        _row_gather(tok_cur_ref, 0, x_hbm, buf_ref.at[0], sem_ref.at[0], EBLOCK)

    @pl.when(i + 1 < n_blocks)
    def _():
        _row_gather(tok_next_ref, 0, x_hbm, buf_ref.at[1 - slot], sem_ref.at[1 - slot], EBLOCK)

    _gather_wait(x_hbm, buf_ref.at[slot], sem_ref.at[slot], EBLOCK)
    xb = buf_ref[slot].astype(BF16)
    hg = _dot(xb, wg_ref[...])
    hu = _dot(xb, wu_ref[...])
    hid = (hg * jax.nn.sigmoid(hg) * hu).astype(BF16)
    o_ref[...] = _dot(hid, wd_ref[...])


def _experts(x2, tok, blk_e, wg16, wu16, wd16):
    N, D = x2.shape
    R = tok.shape[0]
    nb = R // EBLOCK
    tok3 = tok.reshape(nb, 1, EBLOCK)
    grid_spec = pltpu.PrefetchScalarGridSpec(
        num_scalar_prefetch=1,
        grid=(nb,),
        in_specs=[pl.BlockSpec((1, 1, EBLOCK), lambda i, be: (i, 0, 0), memory_space=pltpu.SMEM),
                  pl.BlockSpec((1, 1, EBLOCK), lambda i, be: (jnp.minimum(i + 1, nb - 1), 0, 0),
                               memory_space=pltpu.SMEM),
                  pl.BlockSpec(memory_space=pl.ANY),
                  pl.BlockSpec((None, D, D_EXPERT), lambda i, be: (be[i], 0, 0)),
                  pl.BlockSpec((None, D, D_EXPERT), lambda i, be: (be[i], 0, 0)),
                  pl.BlockSpec((None, D_EXPERT, D), lambda i, be: (be[i], 0, 0))],
        out_specs=pl.BlockSpec((EBLOCK, D), lambda i, be: (i, 0)),
        scratch_shapes=[pltpu.VMEM((2, EBLOCK, D), F32), pltpu.SemaphoreType.DMA((2,))],
    )
    return pl.pallas_call(
        functools.partial(_expert_kernel, n_blocks=nb),
        grid_spec=grid_spec,
        out_shape=jax.ShapeDtypeStruct((R, D), F32),
        compiler_params=_params(("arbitrary",)),
        name="grouped_experts",
    )(blk_e, tok3, tok3, x2, wg16, wu16, wd16)


def _combine_kernel(pos_cur_ref, pos_next_ref, x_ref, gate_ref, g_ref, b_ref, y_hbm, o_ref,
                    buf_ref, sem_ref, *, tm, n_tiles):
    i = pl.program_id(0)
    slot = i % 2

    def gather(pos_ref, sl):
        for k in range(TOP_K):
            _row_gather(pos_ref, k, y_hbm, buf_ref.at[sl, k], sem_ref.at[sl], tm)

    @pl.when(i == 0)
    def _():
        gather(pos_cur_ref, 0)

    @pl.when(i + 1 < n_tiles)
    def _():
        gather(pos_next_ref, 1 - slot)

    for k in range(TOP_K):
        _gather_wait(y_hbm, buf_ref.at[slot, k], sem_ref.at[slot], tm)
    gate = gate_ref[...]
    moe = buf_ref[slot, 0] * gate[:, 0:1] + buf_ref[slot, 1] * gate[:, 1:2]
    h = DN_ALPHA * x_ref[...] + moe
    o_ref[...] = _layer_norm(h, g_ref[...], b_ref[...])


def _combine(x2, ys, pos, gate, g, b):
    N, D = x2.shape
    tm = 256
    nt = N // tm
    pos3 = pos.reshape(nt, tm, TOP_K).transpose(0, 2, 1)
    gate_p = jnp.zeros((N, LANES), F32).at[:, 0:TOP_K].set(gate)
    return pl.pallas_call(
        functools.partial(_combine_kernel, tm=tm, n_tiles=nt),
        grid=(nt,),
        in_specs=[pl.BlockSpec((1, TOP_K, tm), lambda i: (i, 0, 0), memory_space=pltpu.SMEM),
                  pl.BlockSpec((1, TOP_K, tm), lambda i: (jnp.minimum(i + 1, nt - 1), 0, 0),
                               memory_space=pltpu.SMEM),
                  pl.BlockSpec((tm, D), lambda i: (i, 0)),
                  pl.BlockSpec((tm, LANES), lambda i: (i, 0)),
                  pl.BlockSpec((1, D), lambda i: (0, 0)), pl.BlockSpec((1, D), lambda i: (0, 0)),
                  pl.BlockSpec(memory_space=pl.ANY)],
        out_specs=pl.BlockSpec((tm, D), lambda i: (i, 0)),
        out_shape=jax.ShapeDtypeStruct((N, D), F32),
        scratch_shapes=[pltpu.VMEM((2, TOP_K, tm, D), F32), pltpu.SemaphoreType.DMA((2,))],
        compiler_params=_params(("arbitrary",)),
        name="combine_ln3",
    )(pos3, pos3, x2, gate_p, g.reshape(1, D), b.reshape(1, D), ys)


def _dispatch_plan(expert):
    N, K = expert.shape
    A = N * K
    E = N_EXPERTS
    flat_e = expert.reshape(-1)
    order = jnp.argsort(flat_e)
    sorted_e = flat_e[order]
    counts = jnp.bincount(flat_e, length=E)
    padded = (counts + EBLOCK - 1) // EBLOCK * EBLOCK
    pad_end = jnp.cumsum(padded)
    pad_start = pad_end - padded
    start = jnp.cumsum(counts) - counts
    dest = (pad_start[sorted_e] + jnp.arange(A, dtype=jnp.int32) - start[sorted_e]).astype(jnp.int32)
    n_blocks = (A + E * (EBLOCK - 1) + EBLOCK - 1) // EBLOCK
    rows = n_blocks * EBLOCK
    tok = jnp.zeros((rows,), jnp.int32).at[dest].set((order // K).astype(jnp.int32))
    pos = jnp.zeros((A,), jnp.int32).at[order].set(dest).reshape(N, K)
    blk_e = jnp.minimum(jnp.searchsorted(pad_end, jnp.arange(n_blocks) * EBLOCK, side='right'), E - 1)
    return tok, pos, blk_e.astype(jnp.int32)


def _hier_moe_ln3(x2, w_group, b_group, w_router, b_router, wg16, wu16, wd16, g, b):
    expert, gate = _router(x2, w_group, b_group, w_router, b_router)
    tok, pos, blk_e = _dispatch_plan(expert)
    ys = _experts(x2, tok, blk_e, wg16, wu16, wd16)
    return _combine(x2, ys, pos, gate, g, b)


def _layer(x, caches, mem_k, mem_v, lam_init, w):
    B, S, D = x.shape
    N = B * S
    xf = x.reshape(N, D)
    past = 0 if caches is None else caches[0].shape[1]
    (ka, va, kb, vb, sga, sgb, qa16, ka16, va16, qb16, kb16, vb16) = _in_projection(x, w['w_in'], past)
    if caches is None:
        tq = min(512, S)
        tk = min(256, S)
        keys = (ka16, va16, kb16, vb16)
    else:
        tq = S
        tk = 256
        keys = tuple(_pack_keys(c.reshape(B, past, -1), n, tk)
                     for c, n in zip(caches, (ka16, va16, kb16, vb16)))
    oa = _sb_attention(qa16, keys[0], keys[1], tq=tq, tk=tk, pos0=past)
    ob = _df_attention(qb16, keys[2], keys[3], w['lam'], w['subln_g'], tq=tq, tk=tk, pos0=past,
                       lam_init=lam_init)
    x1 = _merge(xf, oa, ob, sga, sgb, w['w_branch_a'], w['w_branch_b'], w['w_out'], w['ln1_g'], w['ln1_b'])
    x2 = _memory_attention(x1, mem_k, mem_v, w['w_mq'], w['w_mo'], w['ln2_g'], w['ln2_b'], batch=B, seq=S)
    x3 = _hier_moe_ln3(x2, w['w_group'], w['b_group'], w['w_router'], w['b_router'],
                       w['w_gate'], w['w_up'], w['w_down'], w['ln3_g'], w['ln3_b'])
    return (x3.reshape(B, S, D), ka.reshape(B, S, SB_HEADS, SB_DIM), va.reshape(B, S, SB_HEADS, SB_DIM),
            kb.reshape(B, S, DF_HEADS, 2 * DF_DIM), vb.reshape(B, S, DF_HEADS, 2 * DF_DIM))


def kernel(x_prompt, mem_prompt, x_sample, cache_sb_k, cache_sb_v, cache_diff_k, cache_diff_v, cache_mem_k, cache_mem_v, w_in, lam_q1, lam_k1, lam_q2, lam_k2, subln_g, w_branch_a, w_branch_b, w_out, ln1_g, ln1_b, w_mq, w_mk, w_mv, w_mo, ln2_g, ln2_b, w_group, b_group, w_router, b_router, w_up, w_gate, w_down, ln3_g, ln3_b):
    depth = w_in.shape[0]
    yp, ys = x_prompt, x_sample
    Bp, n_mem = mem_prompt.shape[0], mem_prompt.shape[1]
    Bs = x_sample.shape[0]
    outs = [[] for _ in range(10)]
    for l in range(depth):
        lam_init = 0.8 - 0.6 * math.exp(-0.3 * l)
        w = {
            'w_in': w_in[l].astype(BF16),
            'lam': jnp.stack([lam_q1[l], lam_k1[l], lam_q2[l], lam_k2[l]]).astype(F32),
            'subln_g': subln_g[l],
            'w_branch_a': w_branch_a[l].astype(BF16), 'w_branch_b': w_branch_b[l].astype(BF16),
            'w_out': w_out[l].astype(BF16), 'ln1_g': ln1_g[l], 'ln1_b': ln1_b[l],
            'w_mq': w_mq[l].astype(BF16), 'w_mo': w_mo[l].astype(BF16),
            'ln2_g': ln2_g[l], 'ln2_b': ln2_b[l],
            'w_group': w_group[l], 'b_group': b_group[l], 'w_router': w_router[l], 'b_router': b_router[l],
            'w_gate': w_gate[l].astype(BF16), 'w_up': w_up[l].astype(BF16), 'w_down': w_down[l].astype(BF16),
            'ln3_g': ln3_g[l], 'ln3_b': ln3_b[l],
        }
        memf = mem_prompt.reshape(Bp * n_mem, D_MODEL)
        mk = _project(memf, w_mk[l].astype(BF16)).reshape(Bp, n_mem, D_MODEL)
        mv = _project(memf, w_mv[l].astype(BF16)).reshape(Bp, n_mem, D_MODEL)
        yp, ka, va, kb, vb = _layer(yp, None, mk, mv, lam_init, w)
        for lst, val in zip(outs[0:6], (ka, va, kb, vb,
                                        mk.reshape(Bp, n_mem, MEM_HEADS, MEM_DIM),
                                        mv.reshape(Bp, n_mem, MEM_HEADS, MEM_DIM))):
            lst.append(val)
        caches = (cache_sb_k[l], cache_sb_v[l], cache_diff_k[l], cache_diff_v[l])
        ys, ka_s, va_s, kb_s, vb_s = _layer(ys, caches,
                                            cache_mem_k[l].reshape(Bs, n_mem, D_MODEL),
                                            cache_mem_v[l].reshape(Bs, n_mem, D_MODEL), lam_init, w)
        for lst, val in zip(outs[6:10], (ka_s, va_s, kb_s, vb_s)):
            lst.append(val)
    return (yp, ys) + tuple(jnp.stack(o) for o in outs)
```

```python
import functools
import math

import jax
import jax.numpy as jnp
import numpy as np
from jax import lax
from jax.experimental import pallas as pl
from jax.experimental.pallas import tpu as pltpu

F32 = jnp.float32
BF16 = jnp.bfloat16

D_MODEL = 1024
SB_HEADS = 8
SB_DIM = 64
SB_W = SB_HEADS * SB_DIM
DF_HEADS = 4
DF_DIM = 64
DF_W = DF_HEADS * 2 * DF_DIM
CHUNK = 64
MEM_HEADS = 4
MEM_DIM = D_MODEL // MEM_HEADS
N_GROUPS = 4
EXPERTS_PER_GROUP = 8
N_EXPERTS = N_GROUPS * EXPERTS_PER_GROUP
TOP_K = 2
D_EXPERT = 512
EBLOCK = 256
ROPE_THETA = 10000.0
LN_EPS = 1e-5
RMS_EPS = 1e-5
DEPTH = 1
DN_ALPHA = (2 * DEPTH) ** 0.25
IN_W = 3 * SB_W + 3 * DF_W + 2 * D_MODEL

LANES = 128
HEAD_GROUPS = 4
NEG_BIG = -1e30
VMEM_LIMIT = 48 * 1024 * 1024


def _dot(a, b):
    return jnp.dot(a, b, preferred_element_type=F32)


def _dot_nt(a, b):
    return lax.dot_general(a, b, (((1,), (1,)), ((), ())), preferred_element_type=F32)


def _layer_norm(h, g, b):
    mu = jnp.mean(h, axis=-1, keepdims=True)
    d = h - mu
    var = jnp.mean(d * d, axis=-1, keepdims=True)
    return d * lax.rsqrt(var + LN_EPS) * g + b


def _params(sem, vmem=VMEM_LIMIT):
    return pltpu.CompilerParams(dimension_semantics=sem, vmem_limit_bytes=vmem)


def _inproj_kernel(x_ref, w_ref, cos_ref, sina_ref, sinb_ref,
                   ka_ref, va_ref, kb_ref, vb_ref, sga_ref, sgb_ref,
                   qa16_ref, ka16_ref, va16_ref, qb16_ref, kb16_ref, vb16_ref):
    xb = x_ref[...].astype(BF16)

    def mm(off, width):
        return _dot(xb, w_ref[:, off:off + width])

    def put_groups(ref, val):
        for g in range(HEAD_GROUPS):
            ref[g] = val[:, g * LANES:(g + 1) * LANES].astype(BF16)

    def rope(t):
        n = t.shape[1]
        reps = n // LANES
        cos = jnp.concatenate([cos_ref[...]] * reps, axis=1)
        sina = jnp.concatenate([sina_ref[...]] * reps, axis=1)
        sinb = jnp.concatenate([sinb_ref[...]] * reps, axis=1)
        fwd = pltpu.roll(t, n - DF_DIM // 2, 1)
        bwd = pltpu.roll(t, DF_DIM // 2, 1)
        return t * cos + fwd * sina + bwd * sinb

    qa = mm(0, SB_W)
    put_groups(qa16_ref, qa * (SB_DIM ** -0.5))
    ka = mm(SB_W, SB_W)
    ka_ref[...] = ka
    put_groups(ka16_ref, ka)
    va = mm(2 * SB_W, SB_W)
    va_ref[...] = va
    put_groups(va16_ref, va)
    off = 3 * SB_W
    qb = rope(mm(off, DF_W))
    put_groups(qb16_ref, qb * (DF_DIM ** -0.5))
    kb = rope(mm(off + DF_W, DF_W))
    kb_ref[...] = kb
    put_groups(kb16_ref, kb)
    vb = mm(off + 2 * DF_W, DF_W)
    vb_ref[...] = vb
    put_groups(vb16_ref, vb)
    off = off + 3 * DF_W
    sga_ref[...] = jax.nn.sigmoid(mm(off, D_MODEL))
    sgb_ref[...] = jax.nn.sigmoid(mm(off + D_MODEL, D_MODEL))


def _rope_tables(pos0, seq):
    half = DF_DIM // 2
    inv_freq = jnp.exp(jnp.arange(half, dtype=F32) * (-math.log(ROPE_THETA) / half))
    ang = (pos0 + jnp.arange(seq, dtype=jnp.int32)).astype(F32)[:, None] * inv_freq[None, :]
    cos, sin = jnp.cos(ang), jnp.sin(ang)
    zero = jnp.zeros_like(sin)
    reps = LANES // DF_DIM
    cos_t = jnp.tile(jnp.concatenate([cos, cos], axis=1), (1, reps))
    sina_t = jnp.tile(jnp.concatenate([-sin, zero], axis=1), (1, reps))
    sinb_t = jnp.tile(jnp.concatenate([zero, sin], axis=1), (1, reps))
    return cos_t, sina_t, sinb_t


def _in_projection(x, w_in16, pos0):
    B, S, D = x.shape
    N = B * S
    tm = min(256, S)
    spb = S // tm
    xf = x.reshape(N, D)
    cos_t, sina_t, sinb_t = _rope_tables(pos0, S)
    row = lambda i: (i, 0)
    tab = lambda i: (i % spb, 0)
    grp = lambda i: (i // spb, 0, i % spb, 0)
    f32_512 = jax.ShapeDtypeStruct((N, SB_W), F32)
    f32_1024 = jax.ShapeDtypeStruct((N, D), F32)
    g16 = jax.ShapeDtypeStruct((B, HEAD_GROUPS, S, LANES), BF16)
    spec512 = pl.BlockSpec((tm, SB_W), row)
    spec1024 = pl.BlockSpec((tm, D), row)
    gspec = pl.BlockSpec((None, HEAD_GROUPS, tm, LANES), grp)
    tspec = pl.BlockSpec((tm, LANES), tab)
    return pl.pallas_call(
        _inproj_kernel,
        grid=(N // tm,),
        in_specs=[spec1024, pl.BlockSpec((D, IN_W), lambda i: (0, 0)), tspec, tspec, tspec],
        out_specs=[spec512, spec512, spec512, spec512, spec1024, spec1024,
                   gspec, gspec, gspec, gspec, gspec, gspec],
        out_shape=[f32_512, f32_512, f32_512, f32_512, f32_1024, f32_1024,
                   g16, g16, g16, g16, g16, g16],
        compiler_params=_params(("parallel",), 56 * 1024 * 1024),
        name="in_projection",
    )(xf, w_in16, cos_t, sina_t, sinb_t)


def _pack_kernel(c_ref, n_ref, o_ref, *, past, new_rows):
    c = c_ref[...]
    for g in range(HEAD_GROUPS):
        o_ref[g, 0:past, :] = c[:, g * LANES:(g + 1) * LANES].astype(BF16)
    pad = o_ref.shape[1] - past - new_rows
    o_ref[:, past:past + new_rows, :] = n_ref[...]
    o_ref[:, past + new_rows:, :] = jnp.zeros((HEAD_GROUPS, pad, LANES), BF16)


def _pack_keys(cache, new16, total):
    B, P, W = cache.shape
    S = new16.shape[2]
    return pl.pallas_call(
        functools.partial(_pack_kernel, past=P, new_rows=S),
        grid=(B,),
        in_specs=[pl.BlockSpec((None, P, W), lambda b: (b, 0, 0)),
                  pl.BlockSpec((None, HEAD_GROUPS, S, LANES), lambda b: (b, 0, 0, 0))],
        out_specs=pl.BlockSpec((None, HEAD_GROUPS, total, LANES), lambda b: (b, 0, 0, 0)),
        out_shape=jax.ShapeDtypeStruct((B, HEAD_GROUPS, total, LANES), BF16),
        compiler_params=_params(("parallel",)),
        name="pack_keys",
    )(cache, new16)


def _schedule(n_q, tq, tk, pos0, fully_visible):
    qi, kb, first, last, masked = [], [], [], [], []
    for i in range(n_q):
        top = -(-(pos0 + (i + 1) * tq) // tk) - 1
        for j in range(top, -1, -1):
            qi.append(i)
            kb.append(j)
            first.append(1 if j == top else 0)
            last.append(1 if j == 0 else 0)
            masked.append(0 if fully_visible(pos0 + i * tq, (j + 1) * tk - 1) else 1)
    mk = lambda v: jnp.asarray(np.asarray(v, np.int32))
    return mk(qi), mk(kb), mk(first), mk(last), mk(masked)


def _half_masks():
    lane = lax.broadcasted_iota(jnp.int32, (1, LANES), 1)
    lo = jnp.where(lane < LANES // 2, 1.0, 0.0).astype(BF16)
    hi = jnp.where(lane >= LANES // 2, 1.0, 0.0).astype(BF16)
    return lo, hi


def _positions(qi, kb, tq, tk, pos0):
    q_pos = pos0 + qi * tq + lax.broadcasted_iota(jnp.int32, (tq, 1), 0)
    k_pos = kb * tk + lax.broadcasted_iota(jnp.int32, (1, tk), 1)
    return q_pos, k_pos


LOG2E = 1.4426950408889634
SIGN_BIT = 0x80000000
BF16_BITS = 0xFFFF0000


def _sb_kernel(qi_ref, kb_ref, first_ref, last_ref, masked_ref, q_ref, k_ref, v_ref, tri_ref, o_ref,
               acc_ref, carry_ref, *, tq, tk, pos0):
    s = pl.program_id(1)
    qi = qi_ref[s]
    kb = kb_ref[s]

    @pl.when(first_ref[s] == 1)
    def _():
        acc_ref[...] = jnp.zeros(acc_ref.shape, F32)
        carry_ref[...] = jnp.zeros(carry_ref.shape, F32)

    reps = tk // LANES

    def tile(use_mask):
        half = _half_masks()
        tri2 = tri_ref[...]
        if use_mask:
            q_pos, k_pos = _positions(qi, kb, tq, tk, pos0)
            mask = k_pos < q_pos

        def group(g, c):
            q = q_ref[g]
            k = k_ref[g]
            v = v_ref[g]
            upd = jnp.zeros((tq, LANES), F32)
            for j in range(2):
                y = _dot_nt(q * half[j], k) * LOG2E
                neg_abs = pltpu.bitcast(pltpu.bitcast(y, jnp.uint32) | jnp.uint32(SIGN_BIT), F32)
                sp = jnp.maximum(y, 0.0) + jnp.log2(1.0 + jnp.exp2(neg_abs))
                if use_mask:
                    sp = jnp.where(mask, sp, 0.0)
                hi = pltpu.bitcast(pltpu.bitcast(sp, jnp.uint32) & jnp.uint32(BF16_BITS), F32)
                lo = sp - hi
                split = jnp.concatenate([hi.astype(BF16), lo.astype(BF16)], axis=1)
                suffix = _dot(split, tri2)
                carry = carry_ref[2 * g + j]
                total = suffix + jnp.concatenate([carry] * reps, axis=1)
                a = jnp.exp2(y - total)
                if use_mask:
                    a = jnp.where(mask, a, 0.0)
                upd = upd + _dot(a.astype(BF16), v * half[j])
                carry_ref[2 * g + j] = carry + jnp.broadcast_to(suffix[:, 0:1], (tq, LANES))
            acc_ref[g] = acc_ref[g] + upd
            return c

        lax.fori_loop(0, HEAD_GROUPS, group, 0, unroll=True)

    @pl.when(masked_ref[s] == 1)
    def _():
        tile(True)

    @pl.when(masked_ref[s] == 0)
    def _():
        tile(False)

    @pl.when(last_ref[s] == 1)
    def _():
        for g in range(HEAD_GROUPS):
            o_ref[:, g * LANES:(g + 1) * LANES] = acc_ref[g].astype(BF16)


def _tri_ones_stacked(tk):
    r = np.arange(tk)
    tri = (r[:, None] >= r[None, :]).astype(np.float32)
    return jnp.asarray(np.concatenate([tri, tri], axis=0)).astype(BF16)


def _sb_attention(q16, k16, v16, *, tq, tk, pos0):
    B, _, S, _ = q16.shape
    n_q = S // tq
    sched = _schedule(n_q, tq, tk, pos0, lambda q_first, k_last: k_last < q_first)
    steps = int(sched[0].shape[0])
    qspec = pl.BlockSpec((None, HEAD_GROUPS, tq, LANES), lambda b, s, qi, kb, f, l, m: (b, 0, qi[s], 0))
    kspec = pl.BlockSpec((None, HEAD_GROUPS, tk, LANES), lambda b, s, qi, kb, f, l, m: (b, 0, kb[s], 0))
    grid_spec = pltpu.PrefetchScalarGridSpec(
        num_scalar_prefetch=5,
        grid=(B, steps),
        in_specs=[qspec, kspec, kspec,
                  pl.BlockSpec((2 * tk, tk), lambda b, s, qi, kb, f, l, m: (0, 0))],
        out_specs=pl.BlockSpec((tq, SB_W), lambda b, s, qi, kb, f, l, m: (b * n_q + qi[s], 0)),
        scratch_shapes=[pltpu.VMEM((HEAD_GROUPS, tq, LANES), F32),
                        pltpu.VMEM((SB_HEADS, tq, LANES), F32)],
    )
    return pl.pallas_call(
        functools.partial(_sb_kernel, tq=tq, tk=tk, pos0=pos0),
        grid_spec=grid_spec,
        out_shape=jax.ShapeDtypeStruct((B * S, SB_W), BF16),
        compiler_params=_params(("parallel", "arbitrary")),
        name="stick_breaking_attention",
    )(*sched, q16, k16, v16, _tri_ones_stacked(tk))


def _df_kernel(qi_ref, kb_ref, first_ref, last_ref, masked_ref, q_ref, k_ref, v_ref, lam_ref, g_ref,
               o_ref, m_ref, l_ref, acc_ref, *, tq, tk, pos0, lam_init):
    s = pl.program_id(1)
    qi = qi_ref[s]
    kb = kb_ref[s]

    @pl.when(first_ref[s] == 1)
    def _():
        m_ref[...] = jnp.full(m_ref.shape, NEG_BIG, F32)
        l_ref[...] = jnp.zeros(l_ref.shape, F32)
        acc_ref[...] = jnp.zeros(acc_ref.shape, F32)

    reps = tk // LANES

    def tile(use_mask):
        half = _half_masks()
        if use_mask:
            q_pos, k_pos = _positions(qi, kb, tq, tk, pos0)
            mask = k_pos <= (q_pos | (CHUNK - 1))

        def head(h, c):
            q = q_ref[h]
            k = k_ref[h]
            v = v_ref[h]
            for j in range(2):
                idx = 2 * h + j
                sc = _dot_nt(q * half[j], k)
                if use_mask:
                    sc = jnp.where(mask, sc, NEG_BIG)
                m_old = m_ref[idx]
                m_new = jnp.maximum(m_old, jnp.max(sc, axis=1, keepdims=True))
                alpha = jnp.exp(m_old - m_new)
                p = jnp.exp(sc - jnp.concatenate([m_new] * reps, axis=1))
                l_ref[idx] = alpha * l_ref[idx] + jnp.sum(p, axis=1, keepdims=True)
                acc_ref[idx] = alpha * acc_ref[idx] + _dot(p.astype(BF16), v)
                m_ref[idx] = m_new
            return c

        lax.fori_loop(0, DF_HEADS, head, 0, unroll=True)

    @pl.when(masked_ref[s] == 1)
    def _():
        tile(True)

    @pl.when(masked_ref[s] == 0)
    def _():
        tile(False)

    @pl.when(last_ref[s] == 1)
    def _():
        lv = lam_ref[...]
        lam = (jnp.exp(jnp.sum(lv[0:1] * lv[1:2], axis=1, keepdims=True))
               - jnp.exp(jnp.sum(lv[2:3] * lv[3:4], axis=1, keepdims=True)) + lam_init)
        gain = g_ref[...] * (1.0 - lam_init)
        for h in range(DF_HEADS):
            o = acc_ref[2 * h] / l_ref[2 * h] - lam * (acc_ref[2 * h + 1] / l_ref[2 * h + 1])
            o = o * lax.rsqrt(jnp.mean(o * o, axis=-1, keepdims=True) + RMS_EPS)
            o_ref[:, h * LANES:(h + 1) * LANES] = (o * gain).astype(BF16)


def _df_attention(q16, k16, v16, lam_vecs, subln_g, *, tq, tk, pos0, lam_init):
    B, _, S, _ = q16.shape
    n_q = S // tq
    sched = _schedule(n_q, tq, tk, pos0, lambda q_first, k_last: k_last <= (q_first | (CHUNK - 1)))
    steps = int(sched[0].shape[0])
    qspec = pl.BlockSpec((None, DF_HEADS, tq, LANES), lambda b, s, qi, kb, f, l, m: (b, 0, qi[s], 0))
    kspec = pl.BlockSpec((None, DF_HEADS, tk, LANES), lambda b, s, qi, kb, f, l, m: (b, 0, kb[s], 0))
    const = lambda b, s, qi, kb, f, l, m: (0, 0)
    grid_spec = pltpu.PrefetchScalarGridSpec(
        num_scalar_prefetch=5,
        grid=(B, steps),
        in_specs=[qspec, kspec, kspec,
                  pl.BlockSpec((4, DF_DIM), const), pl.BlockSpec((1, LANES), const)],
        out_specs=pl.BlockSpec((tq, DF_W), lambda b, s, qi, kb, f, l, m: (b * n_q + qi[s], 0)),
        scratch_shapes=[pltpu.VMEM((2 * DF_HEADS, tq, LANES), F32),
                        pltpu.VMEM((2 * DF_HEADS, tq, LANES), F32),
                        pltpu.VMEM((2 * DF_HEADS, tq, LANES), F32)],
    )
    return pl.pallas_call(
        functools.partial(_df_kernel, tq=tq, tk=tk, pos0=pos0, lam_init=lam_init),
        grid_spec=grid_spec,
        out_shape=jax.ShapeDtypeStruct((B * S, DF_W), BF16),
        compiler_params=_params(("parallel", "arbitrary")),
        name="differential_attention",
    )(*sched, q16, k16, v16, lam_vecs, subln_g.reshape(1, LANES))


def _merge_kernel(x_ref, oa_ref, ob_ref, sga_ref, sgb_ref, wa_ref, wb_ref, wo_ref, g_ref, b_ref, o_ref):
    ya = _dot(oa_ref[...], wa_ref[...])
    yb = _dot(ob_ref[...], wb_ref[...])
    mix = (sga_ref[...] * ya + sgb_ref[...] * yb).astype(BF16)
    h = DN_ALPHA * x_ref[...] + _dot(mix, wo_ref[...])
    o_ref[...] = _layer_norm(h, g_ref[...], b_ref[...])


def _merge(xf, oa, ob, sga, sgb, wa16, wb16, wo16, g, b):
    N, D = xf.shape
    tm = 256
    row = lambda i: (i, 0)
    const = lambda i: (0, 0)
    return pl.pallas_call(
        _merge_kernel,
        grid=(N // tm,),
        in_specs=[pl.BlockSpec((tm, D), row), pl.BlockSpec((tm, SB_W), row), pl.BlockSpec((tm, DF_W), row),
                  pl.BlockSpec((tm, D), row), pl.BlockSpec((tm, D), row),
                  pl.BlockSpec((SB_W, D), const), pl.BlockSpec((DF_W, D), const), pl.BlockSpec((D, D), const),
                  pl.BlockSpec((1, D), const), pl.BlockSpec((1, D), const)],
        out_specs=pl.BlockSpec((tm, D), row),
        out_shape=jax.ShapeDtypeStruct((N, D), F32),
        compiler_params=_params(("parallel",)),
        name="merge_ln1",
    )(xf, oa, ob, sga, sgb, wa16, wb16, wo16, g.reshape(1, D), b.reshape(1, D))


def _proj_kernel(x_ref, w_ref, o_ref):
    o_ref[...] = _dot(x_ref[...].astype(BF16), w_ref[...])


def _project(xf, w16):
    N, D = xf.shape
    tm = 256
    return pl.pallas_call(
        _proj_kernel,
        grid=(N // tm,),
        in_specs=[pl.BlockSpec((tm, D), lambda i: (i, 0)), pl.BlockSpec(w16.shape, lambda i: (0, 0))],
        out_specs=pl.BlockSpec((tm, w16.shape[1]), lambda i: (i, 0)),
        out_shape=jax.ShapeDtypeStruct((N, w16.shape[1]), F32),
        compiler_params=_params(("parallel",)),
        name="memory_projection",
    )(xf, w16)


def _memattn_kernel(x_ref, mk_ref, mv_ref, wq_ref, wo_ref, g_ref, b_ref, o_ref):
    x = x_ref[...]
    q = (_dot(x.astype(BF16), wq_ref[...]) * (MEM_DIM ** -0.5)).astype(BF16)
    outs = []
    for h in range(MEM_HEADS):
        sl = slice(h * MEM_DIM, (h + 1) * MEM_DIM)
        sc = _dot_nt(q[:, sl], mk_ref[:, sl].astype(BF16))
        sc = sc - jnp.max(sc, axis=-1, keepdims=True)
        p = jnp.exp(sc)
        p = p / jnp.sum(p, axis=-1, keepdims=True)
        outs.append(_dot(p.astype(BF16), mv_ref[:, sl].astype(BF16)).astype(BF16))
    o = jnp.concatenate(outs, axis=1)
    h2 = DN_ALPHA * x + _dot(o, wo_ref[...])
    o_ref[...] = _layer_norm(h2, g_ref[...], b_ref[...])


def _memory_attention(x1, mem_k, mem_v, wq16, wo16, g, b, *, batch, seq):
    N, D = x1.shape
    M = mem_k.shape[1]
    tm = min(256, seq)
    spb = seq // tm
    const = lambda bi, i: (0, 0)
    return pl.pallas_call(
        _memattn_kernel,
        grid=(batch, spb),
        in_specs=[pl.BlockSpec((tm, D), lambda bi, i: (bi * spb + i, 0)),
                  pl.BlockSpec((None, M, D), lambda bi, i: (bi, 0, 0)),
                  pl.BlockSpec((None, M, D), lambda bi, i: (bi, 0, 0)),
                  pl.BlockSpec((D, D), const), pl.BlockSpec((D, D), const),
                  pl.BlockSpec((1, D), const), pl.BlockSpec((1, D), const)],
        out_specs=pl.BlockSpec((tm, D), lambda bi, i: (bi * spb + i, 0)),
        out_shape=jax.ShapeDtypeStruct((N, D), F32),
        compiler_params=_params(("parallel", "arbitrary")),
        name="memory_attention_ln2",
    )(x1, mem_k, mem_v, wq16, wo16, g.reshape(1, D), b.reshape(1, D))


def _router_kernel(x_ref, w_ref, b_ref, o_ref):
    x = x_ref[...]
    tm = x.shape[0]
    x_hi = x.astype(BF16)
    x_lo = (x - x_hi.astype(F32)).astype(BF16)
    w = w_ref[...]
    w_hi = w.astype(BF16)
    w_lo = (w - w_hi.astype(F32)).astype(BF16)
    logits = _dot(x_hi, w_hi) + _dot(x_lo, w_hi) + _dot(x_hi, w_lo) + b_ref[...]
    lane = lax.broadcasted_iota(jnp.int32, (tm, LANES), 1)
    big = jnp.int32(LANES)
    glog = jnp.where(lane < N_GROUPS, logits, -jnp.inf)
    gmax = jnp.max(glog, axis=-1, keepdims=True)
    grp = jnp.min(jnp.where(glog == gmax, lane, big), axis=-1, keepdims=True)
    g_prob = 1.0 / jnp.sum(jnp.exp(glog - gmax), axis=-1, keepdims=True)
    lo = N_GROUPS + grp * EXPERTS_PER_GROUP
    in_grp = (lane >= lo) & (lane < lo + EXPERTS_PER_GROUP)
    el = jnp.where(in_grp, logits, -jnp.inf)
    v1 = jnp.max(el, axis=-1, keepdims=True)
    i1 = jnp.min(jnp.where(el == v1, lane, big), axis=-1, keepdims=True)
    el2 = jnp.where(lane == i1, -jnp.inf, el)
    v2 = jnp.max(el2, axis=-1, keepdims=True)
    i2 = jnp.min(jnp.where(el2 == v2, lane, big), axis=-1, keepdims=True)
    e2 = jnp.exp(v2 - v1)
    den = 1.0 / (1.0 + e2)
    g1 = den * g_prob
    g2 = e2 * den * g_prob
    out = jnp.where(lane == 0, (i1 - N_GROUPS).astype(F32), 0.0)
    out = jnp.where(lane == 1, (i2 - N_GROUPS).astype(F32), out)
    out = jnp.where(lane == 2, g1, out)
    out = jnp.where(lane == 3, g2, out)
    o_ref[...] = out


def _router(x2, w_group, b_group, w_router, b_router):
    N, D = x2.shape
    tm = 256
    w = jnp.zeros((D, LANES), F32).at[:, :N_GROUPS].set(w_group).at[:, N_GROUPS:N_GROUPS + N_EXPERTS].set(w_router)
    b = jnp.zeros((1, LANES), F32).at[0, :N_GROUPS].set(b_group).at[0, N_GROUPS:N_GROUPS + N_EXPERTS].set(b_router)
    r = pl.pallas_call(
        _router_kernel,
        grid=(N // tm,),
        in_specs=[pl.BlockSpec((tm, D), lambda i: (i, 0)), pl.BlockSpec((D, LANES), lambda i: (0, 0)),
                  pl.BlockSpec((1, LANES), lambda i: (0, 0))],
        out_specs=pl.BlockSpec((tm, LANES), lambda i: (i, 0)),
        out_shape=jax.ShapeDtypeStruct((N, LANES), F32),
        compiler_params=_params(("parallel",)),
        name="router",
    )(x2, w, b)
    expert = r[:, 0:2].astype(jnp.int32)
    gate = r[:, 2:4]
    return expert, gate


def _row_gather(idx_ref, k, src_hbm, dst_ref, sem, rows):
    for r in range(rows):
        t = idx_ref[0, k, r]
        pltpu.make_async_copy(src_hbm.at[pl.ds(t, 1)], dst_ref.at[pl.ds(r, 1)], sem).start(priority=r % 2)


def _gather_wait(src_hbm, dst_ref, sem, rows):
    pltpu.make_async_copy(src_hbm.at[pl.ds(0, rows)], dst_ref, sem).wait()


def _expert_kernel(be_ref, tok_cur_ref, tok_next_ref, x_hbm, wg_ref, wu_ref, wd_ref, o_ref,
                   buf_ref, sem_ref, *, n_blocks):
    i = pl.program_id(0)
    slot = i % 2

    @pl.when(i == 0)
    def _():
        _row_gather(tok_cur_ref, 0, x_hbm, buf_ref.at[0], sem_ref.at[0], EBLOCK)

    _gather_wait(x_hbm, buf_ref.at[slot], sem_ref.at[slot], EBLOCK)
    _row_gather(tok_next_ref, 0, x_hbm, buf_ref.at[1 - slot], sem_ref.at[1 - slot], EBLOCK)
    xb = buf_ref[slot].astype(BF16)
    hg = _dot(xb, wg_ref[...])
    hu = _dot(xb, wu_ref[...])
    hid = (hg * jax.nn.sigmoid(hg) * hu).astype(BF16)
    o_ref[...] = _dot(hid, wd_ref[...])

    @pl.when(i == n_blocks - 1)
    def _():
        _gather_wait(x_hbm, buf_ref.at[1 - slot], sem_ref.at[1 - slot], EBLOCK)


def _experts(x2, tok, blk_e, wg16, wu16, wd16):
    N, D = x2.shape
    R = tok.shape[0]
    nb = R // EBLOCK
    tok3 = tok.reshape(nb, 1, EBLOCK)
    grid_spec = pltpu.PrefetchScalarGridSpec(
        num_scalar_prefetch=1,
        grid=(nb,),
        in_specs=[pl.BlockSpec((1, 1, EBLOCK), lambda i, be: (i, 0, 0), memory_space=pltpu.SMEM),
                  pl.BlockSpec((1, 1, EBLOCK), lambda i, be: (jnp.minimum(i + 1, nb - 1), 0, 0),
                               memory_space=pltpu.SMEM),
                  pl.BlockSpec(memory_space=pl.ANY),
                  pl.BlockSpec((None, D, D_EXPERT), lambda i, be: (be[i], 0, 0)),
                  pl.BlockSpec((None, D, D_EXPERT), lambda i, be: (be[i], 0, 0)),
                  pl.BlockSpec((None, D_EXPERT, D), lambda i, be: (be[i], 0, 0))],
        out_specs=pl.BlockSpec((EBLOCK, D), lambda i, be: (i, 0)),
        scratch_shapes=[pltpu.VMEM((2, EBLOCK, D), F32), pltpu.SemaphoreType.DMA((2,))],
    )
    return pl.pallas_call(
        functools.partial(_expert_kernel, n_blocks=nb),
        grid_spec=grid_spec,
        out_shape=jax.ShapeDtypeStruct((R, D), F32),
        compiler_params=_params(("arbitrary",)),
        name="grouped_experts",
    )(blk_e, tok3, tok3, x2, wg16, wu16, wd16)


def _combine_kernel(pos_cur_ref, pos_next_ref, x_ref, gate_ref, g_ref, b_ref, y_hbm, o_ref,
                    buf_ref, sem_ref, *, tm, n_tiles):
    i = pl.program_id(0)
    slot = i % 2

    def gather(pos_ref, sl):
        for k in range(TOP_K):
            _row_gather(pos_ref, k, y_hbm, buf_ref.at[sl, k], sem_ref.at[sl], tm)

    def wait(sl):
        for k in range(TOP_K):
            _gather_wait(y_hbm, buf_ref.at[sl, k], sem_ref.at[sl], tm)

    @pl.when(i == 0)
    def _():
        gather(pos_cur_ref, 0)

    wait(slot)
    gather(pos_next_ref, 1 - slot)
    gate = gate_ref[...]
    moe = buf_ref[slot, 0] * gate[:, 0:1] + buf_ref[slot, 1] * gate[:, 1:2]
    h = DN_ALPHA * x_ref[...] + moe
    o_ref[...] = _layer_norm(h, g_ref[...], b_ref[...])

    @pl.when(i == n_tiles - 1)
    def _():
        wait(1 - slot)


def _combine(x2, ys, pos, gate, g, b):
    N, D = x2.shape
    tm = 256
    nt = N // tm
    pos3 = pos.reshape(nt, tm, TOP_K).transpose(0, 2, 1)
    gate_p = jnp.zeros((N, LANES), F32).at[:, 0:TOP_K].set(gate)
    return pl.pallas_call(
        functools.partial(_combine_kernel, tm=tm, n_tiles=nt),
        grid=(nt,),
        in_specs=[pl.BlockSpec((1, TOP_K, tm), lambda i: (i, 0, 0), memory_space=pltpu.SMEM),
                  pl.BlockSpec((1, TOP_K, tm), lambda i: (jnp.minimum(i + 1, nt - 1), 0, 0),
                               memory_space=pltpu.SMEM),
                  pl.BlockSpec((tm, D), lambda i: (i, 0)),
                  pl.BlockSpec((tm, LANES), lambda i: (i, 0)),
                  pl.BlockSpec((1, D), lambda i: (0, 0)), pl.BlockSpec((1, D), lambda i: (0, 0)),
                  pl.BlockSpec(memory_space=pl.ANY)],
        out_specs=pl.BlockSpec((tm, D), lambda i: (i, 0)),
        out_shape=jax.ShapeDtypeStruct((N, D), F32),
        scratch_shapes=[pltpu.VMEM((2, TOP_K, tm, D), F32), pltpu.SemaphoreType.DMA((2,))],
        compiler_params=_params(("arbitrary",)),
        name="combine_ln3",
    )(pos3, pos3, x2, gate_p, g.reshape(1, D), b.reshape(1, D), ys)


def _dispatch_plan(expert):
    N, K = expert.shape
    A = N * K
    E = N_EXPERTS
    flat_e = expert.reshape(-1)
    order = jnp.argsort(flat_e)
    sorted_e = flat_e[order]
    counts = jnp.bincount(flat_e, length=E)
    padded = (counts + EBLOCK - 1) // EBLOCK * EBLOCK
    pad_end = jnp.cumsum(padded)
    pad_start = pad_end - padded
    start = jnp.cumsum(counts) - counts
    dest = (pad_start[sorted_e] + jnp.arange(A, dtype=jnp.int32) - start[sorted_e]).astype(jnp.int32)
    n_blocks = (A + E * (EBLOCK - 1) + EBLOCK - 1) // EBLOCK
    rows = n_blocks * EBLOCK
    tok = jnp.zeros((rows,), jnp.int32).at[dest].set((order // K).astype(jnp.int32))
    pos = jnp.zeros((A,), jnp.int32).at[order].set(dest).reshape(N, K)
    blk_e = jnp.minimum(jnp.searchsorted(pad_end, jnp.arange(n_blocks) * EBLOCK, side='right'), E - 1)
    return tok, pos, blk_e.astype(jnp.int32)


def _hier_moe_ln3(x2, w_group, b_group, w_router, b_router, wg16, wu16, wd16, g, b):
    expert, gate = _router(x2, w_group, b_group, w_router, b_router)
    tok, pos, blk_e = _dispatch_plan(expert)
    ys = _experts(x2, tok, blk_e, wg16, wu16, wd16)
    return _combine(x2, ys, pos, gate, g, b)


def _layer(x, caches, mem_k, mem_v, lam_init, w):
    B, S, D = x.shape
    N = B * S
    xf = x.reshape(N, D)
    past = 0 if caches is None else caches[0].shape[1]
    (ka, va, kb, vb, sga, sgb, qa16, ka16, va16, qb16, kb16, vb16) = _in_projection(x, w['w_in'], past)
    if caches is None:
        tq = min(512, S)
        tk = min(256, S)
        keys = (ka16, va16, kb16, vb16)
    else:
        tq = S
        tk = 256
        keys = tuple(_pack_keys(c.reshape(B, past, -1), n, past + tk)
                     for c, n in zip(caches, (ka16, va16, kb16, vb16)))
    oa = _sb_attention(qa16, keys[0], keys[1], tq=tq, tk=tk, pos0=past)
    tk_df = tk if caches is not None else min(1024, S)
    ob = _df_attention(qb16, keys[2], keys[3], w['lam'], w['subln_g'], tq=tq, tk=tk_df, pos0=past,
                       lam_init=lam_init)
    x1 = _merge(xf, oa, ob, sga, sgb, w['w_branch_a'], w['w_branch_b'], w['w_out'], w['ln1_g'], w['ln1_b'])
    x2 = _memory_attention(x1, mem_k, mem_v, w['w_mq'], w['w_mo'], w['ln2_g'], w['ln2_b'], batch=B, seq=S)
    x3 = _hier_moe_ln3(x2, w['w_group'], w['b_group'], w['w_router'], w['b_router'],
                       w['w_gate'], w['w_up'], w['w_down'], w['ln3_g'], w['ln3_b'])
    return (x3.reshape(B, S, D), ka.reshape(B, S, SB_HEADS, SB_DIM), va.reshape(B, S, SB_HEADS, SB_DIM),
            kb.reshape(B, S, DF_HEADS, 2 * DF_DIM), vb.reshape(B, S, DF_HEADS, 2 * DF_DIM))


def kernel(x_prompt, mem_prompt, x_sample, cache_sb_k, cache_sb_v, cache_diff_k, cache_diff_v, cache_mem_k, cache_mem_v, w_in, lam_q1, lam_k1, lam_q2, lam_k2, subln_g, w_branch_a, w_branch_b, w_out, ln1_g, ln1_b, w_mq, w_mk, w_mv, w_mo, ln2_g, ln2_b, w_group, b_group, w_router, b_router, w_up, w_gate, w_down, ln3_g, ln3_b):
    depth = w_in.shape[0]
    yp, ys = x_prompt, x_sample
    Bp, n_mem = mem_prompt.shape[0], mem_prompt.shape[1]
    Bs = x_sample.shape[0]
    outs = [[] for _ in range(10)]
    for l in range(depth):
        lam_init = 0.8 - 0.6 * math.exp(-0.3 * l)
        w = {
            'w_in': w_in[l].astype(BF16),
            'lam': jnp.stack([lam_q1[l], lam_k1[l], lam_q2[l], lam_k2[l]]).astype(F32),
            'subln_g': subln_g[l],
            'w_branch_a': w_branch_a[l].astype(BF16), 'w_branch_b': w_branch_b[l].astype(BF16),
            'w_out': w_out[l].astype(BF16), 'ln1_g': ln1_g[l], 'ln1_b': ln1_b[l],
            'w_mq': w_mq[l].astype(BF16), 'w_mo': w_mo[l].astype(BF16),
            'ln2_g': ln2_g[l], 'ln2_b': ln2_b[l],
            'w_group': w_group[l], 'b_group': b_group[l], 'w_router': w_router[l], 'b_router': b_router[l],
            'w_gate': w_gate[l].astype(BF16), 'w_up': w_up[l].astype(BF16), 'w_down': w_down[l].astype(BF16),
            'ln3_g': ln3_g[l], 'ln3_b': ln3_b[l],
        }
        memf = mem_prompt.reshape(Bp * n_mem, D_MODEL)
        mk = _project(memf, w_mk[l].astype(BF16)).reshape(Bp, n_mem, D_MODEL)
        mv = _project(memf, w_mv[l].astype(BF16)).reshape(Bp, n_mem, D_MODEL)
        yp, ka, va, kb, vb = _layer(yp, None, mk, mv, lam_init, w)
        for lst, val in zip(outs[0:6], (ka, va, kb, vb,
                                        mk.reshape(Bp, n_mem, MEM_HEADS, MEM_DIM),
                                        mv.reshape(Bp, n_mem, MEM_HEADS, MEM_DIM))):
            lst.append(val)
        caches = (cache_sb_k[l], cache_sb_v[l], cache_diff_k[l], cache_diff_v[l])
        ys, ka_s, va_s, kb_s, vb_s = _layer(ys, caches,
                                            cache_mem_k[l].reshape(Bs, n_mem, D_MODEL),
                                            cache_mem_v[l].reshape(Bs, n_mem, D_MODEL), lam_init, w)
        for lst, val in zip(outs[6:10], (ka_s, va_s, kb_s, vb_s)):
            lst.append(val)
    return (yp, ys) + tuple(jnp.stack(o) for o in outs)
```

```python
import functools
import math

import jax
import jax.numpy as jnp
import numpy as np
from jax import lax
from jax.experimental import pallas as pl
from jax.experimental.pallas import tpu as pltpu

F32 = jnp.float32
BF16 = jnp.bfloat16

D_MODEL = 1024
SB_HEADS = 8
SB_DIM = 64
SB_W = SB_HEADS * SB_DIM
DF_HEADS = 4
DF_DIM = 64
DF_W = DF_HEADS * 2 * DF_DIM
CHUNK = 64
MEM_HEADS = 4
MEM_DIM = D_MODEL // MEM_HEADS
N_GROUPS = 4
EXPERTS_PER_GROUP = 8
N_EXPERTS = N_GROUPS * EXPERTS_PER_GROUP
TOP_K = 2
D_EXPERT = 512
EBLOCK = 256
ROPE_THETA = 10000.0
LN_EPS = 1e-5
RMS_EPS = 1e-5
DEPTH = 1
DN_ALPHA = (2 * DEPTH) ** 0.25
IN_W = 3 * SB_W + 3 * DF_W + 2 * D_MODEL

LANES = 128
HEAD_GROUPS = 4
NEG_BIG = -1e30
VMEM_LIMIT = 48 * 1024 * 1024


def _dot(a, b):
    return jnp.dot(a, b, preferred_element_type=F32)


def _dot_nt(a, b):
    return lax.dot_general(a, b, (((1,), (1,)), ((), ())), preferred_element_type=F32)


def _layer_norm(h, g, b):
    mu = jnp.mean(h, axis=-1, keepdims=True)
    d = h - mu
    var = jnp.mean(d * d, axis=-1, keepdims=True)
    return d * lax.rsqrt(var + LN_EPS) * g + b


def _params(sem, vmem=VMEM_LIMIT):
    return pltpu.CompilerParams(dimension_semantics=sem, vmem_limit_bytes=vmem)


def _inproj_kernel(x_ref, w_ref, cos_ref, sina_ref, sinb_ref,
                   ka_ref, va_ref, kb_ref, vb_ref, sga_ref, sgb_ref,
                   qa16_ref, ka16_ref, va16_ref, qb16_ref, kb16_ref, vb16_ref):
    xb = x_ref[...].astype(BF16)

    def mm(off, width):
        return _dot(xb, w_ref[:, off:off + width])

    def put_groups(ref, val):
        for g in range(HEAD_GROUPS):
            ref[g] = val[:, g * LANES:(g + 1) * LANES].astype(BF16)

    def rope(t):
        n = t.shape[1]
        reps = n // LANES
        cos = jnp.concatenate([cos_ref[...]] * reps, axis=1)
        sina = jnp.concatenate([sina_ref[...]] * reps, axis=1)
        sinb = jnp.concatenate([sinb_ref[...]] * reps, axis=1)
        fwd = pltpu.roll(t, n - DF_DIM // 2, 1)
        bwd = pltpu.roll(t, DF_DIM // 2, 1)
        return t * cos + fwd * sina + bwd * sinb

    qa = mm(0, SB_W)
    put_groups(qa16_ref, qa * (SB_DIM ** -0.5))
    ka = mm(SB_W, SB_W)
    ka_ref[...] = ka
    put_groups(ka16_ref, ka)
    va = mm(2 * SB_W, SB_W)
    va_ref[...] = va
    put_groups(va16_ref, va)
    off = 3 * SB_W
    qb = rope(mm(off, DF_W))
    put_groups(qb16_ref, qb * (DF_DIM ** -0.5))
    kb = rope(mm(off + DF_W, DF_W))
    kb_ref[...] = kb
    put_groups(kb16_ref, kb)
    vb = mm(off + 2 * DF_W, DF_W)
    vb_ref[...] = vb
    put_groups(vb16_ref, vb)
    off = off + 3 * DF_W
    sga_ref[...] = jax.nn.sigmoid(mm(off, D_MODEL))
    sgb_ref[...] = jax.nn.sigmoid(mm(off + D_MODEL, D_MODEL))


def _rope_tables(pos0, seq):
    half = DF_DIM // 2
    inv_freq = jnp.exp(jnp.arange(half, dtype=F32) * (-math.log(ROPE_THETA) / half))
    ang = (pos0 + jnp.arange(seq, dtype=jnp.int32)).astype(F32)[:, None] * inv_freq[None, :]
    cos, sin = jnp.cos(ang), jnp.sin(ang)
    zero = jnp.zeros_like(sin)
    reps = LANES // DF_DIM
    cos_t = jnp.tile(jnp.concatenate([cos, cos], axis=1), (1, reps))
    sina_t = jnp.tile(jnp.concatenate([-sin, zero], axis=1), (1, reps))
    sinb_t = jnp.tile(jnp.concatenate([zero, sin], axis=1), (1, reps))
    return cos_t, sina_t, sinb_t


def _in_projection(x, w_in16, pos0):
    B, S, D = x.shape
    N = B * S
    tm = min(256, S)
    spb = S // tm
    xf = x.reshape(N, D)
    cos_t, sina_t, sinb_t = _rope_tables(pos0, S)
    row = lambda i: (i, 0)
    tab = lambda i: (i % spb, 0)
    grp = lambda i: (i // spb, 0, i % spb, 0)
    f32_512 = jax.ShapeDtypeStruct((N, SB_W), F32)
    f32_1024 = jax.ShapeDtypeStruct((N, D), F32)
    g16 = jax.ShapeDtypeStruct((B, HEAD_GROUPS, S, LANES), BF16)
    spec512 = pl.BlockSpec((tm, SB_W), row)
    spec1024 = pl.BlockSpec((tm, D), row)
    gspec = pl.BlockSpec((None, HEAD_GROUPS, tm, LANES), grp)
    tspec = pl.BlockSpec((tm, LANES), tab)
    return pl.pallas_call(
        _inproj_kernel,
        grid=(N // tm,),
        in_specs=[spec1024, pl.BlockSpec((D, IN_W), lambda i: (0, 0)), tspec, tspec, tspec],
        out_specs=[spec512, spec512, spec512, spec512, spec1024, spec1024,
                   gspec, gspec, gspec, gspec, gspec, gspec],
        out_shape=[f32_512, f32_512, f32_512, f32_512, f32_1024, f32_1024,
                   g16, g16, g16, g16, g16, g16],
        compiler_params=_params(("parallel",), 56 * 1024 * 1024),
        name="in_projection",
    )(xf, w_in16, cos_t, sina_t, sinb_t)


def _pack_kernel(c_ref, n_ref, o_ref, *, past, new_rows):
    c = c_ref[...]
    for g in range(HEAD_GROUPS):
        o_ref[g, 0:past, :] = c[:, g * LANES:(g + 1) * LANES].astype(BF16)
    pad = o_ref.shape[1] - past - new_rows
    o_ref[:, past:past + new_rows, :] = n_ref[...]
    o_ref[:, past + new_rows:, :] = jnp.zeros((HEAD_GROUPS, pad, LANES), BF16)


def _pack_keys(cache, new16, total):
    B, P, W = cache.shape
    S = new16.shape[2]
    return pl.pallas_call(
        functools.partial(_pack_kernel, past=P, new_rows=S),
        grid=(B,),
        in_specs=[pl.BlockSpec((None, P, W), lambda b: (b, 0, 0)),
                  pl.BlockSpec((None, HEAD_GROUPS, S, LANES), lambda b: (b, 0, 0, 0))],
        out_specs=pl.BlockSpec((None, HEAD_GROUPS, total, LANES), lambda b: (b, 0, 0, 0)),
        out_shape=jax.ShapeDtypeStruct((B, HEAD_GROUPS, total, LANES), BF16),
        compiler_params=_params(("parallel",)),
        name="pack_keys",
    )(cache, new16)


def _schedule(n_q, tq, tk, pos0, fully_visible):
    qi, kb, first, last, masked = [], [], [], [], []
    for i in range(n_q):
        top = -(-(pos0 + (i + 1) * tq) // tk) - 1
        for j in range(top, -1, -1):
            qi.append(i)
            kb.append(j)
            first.append(1 if j == top else 0)
            last.append(1 if j == 0 else 0)
            masked.append(0 if fully_visible(pos0 + i * tq, (j + 1) * tk - 1) else 1)
    mk = lambda v: jnp.asarray(np.asarray(v, np.int32))
    return mk(qi), mk(kb), mk(first), mk(last), mk(masked)


def _half_masks():
    lane = lax.broadcasted_iota(jnp.int32, (1, LANES), 1)
    lo = jnp.where(lane < LANES // 2, 1.0, 0.0).astype(BF16)
    hi = jnp.where(lane >= LANES // 2, 1.0, 0.0).astype(BF16)
    return lo, hi


def _positions(qi, kb, tq, tk, pos0):
    q_pos = pos0 + qi * tq + lax.broadcasted_iota(jnp.int32, (tq, 1), 0)
    k_pos = kb * tk + lax.broadcasted_iota(jnp.int32, (1, tk), 1)
    return q_pos, k_pos


LOG2E = 1.4426950408889634
SIGN_BIT = 0x80000000
BF16_BITS = 0xFFFF0000


def _sb_kernel(qi_ref, kb_ref, first_ref, last_ref, masked_ref, q_ref, k_ref, v_ref, tri_ref, o_ref,
               acc_ref, carry_ref, *, tq, tk, pos0):
    s = pl.program_id(1)
    qi = qi_ref[s]
    kb = kb_ref[s]

    @pl.when(first_ref[s] == 1)
    def _():
        acc_ref[...] = jnp.zeros(acc_ref.shape, F32)
        carry_ref[...] = jnp.zeros(carry_ref.shape, F32)

    reps = tk // LANES

    def tile(use_mask):
        half = _half_masks()
        tri2 = tri_ref[...]
        if use_mask:
            q_pos, k_pos = _positions(qi, kb, tq, tk, pos0)
            mask = k_pos < q_pos

        def group(g, c):
            q = q_ref[g]
            k = k_ref[g]
            v = v_ref[g]
            upd = jnp.zeros((tq, LANES), F32)
            for j in range(2):
                y = _dot_nt(q * half[j], k) * LOG2E
                neg_abs = pltpu.bitcast(pltpu.bitcast(y, jnp.uint32) | jnp.uint32(SIGN_BIT), F32)
                sp = jnp.maximum(y, 0.0) + jnp.log2(1.0 + jnp.exp2(neg_abs))
                if use_mask:
                    sp = jnp.where(mask, sp, 0.0)
                hi = pltpu.bitcast(pltpu.bitcast(sp, jnp.uint32) & jnp.uint32(BF16_BITS), F32)
                lo = sp - hi
                split = jnp.concatenate([hi.astype(BF16), lo.astype(BF16)], axis=1)
                suffix = _dot(split, tri2)
                carry = carry_ref[2 * g + j]
                total = suffix + jnp.concatenate([carry] * reps, axis=1)
                a = jnp.exp2(y - total)
                if use_mask:
                    a = jnp.where(mask, a, 0.0)
                upd = upd + _dot(a.astype(BF16), v * half[j])
                carry_ref[2 * g + j] = carry + jnp.broadcast_to(suffix[:, 0:1], (tq, LANES))
            acc_ref[g] = acc_ref[g] + upd
            return c

        lax.fori_loop(0, HEAD_GROUPS, group, 0, unroll=True)

    @pl.when(masked_ref[s] == 1)
    def _():
        tile(True)

    @pl.when(masked_ref[s] == 0)
    def _():
        tile(False)

    @pl.when(last_ref[s] == 1)
    def _():
        for g in range(HEAD_GROUPS):
            o_ref[:, g * LANES:(g + 1) * LANES] = acc_ref[g].astype(BF16)


def _tri_ones_stacked(tk):
    r = np.arange(tk)
    tri = (r[:, None] >= r[None, :]).astype(np.float32)
    return jnp.asarray(np.concatenate([tri, tri], axis=0)).astype(BF16)


def _sb_attention(q16, k16, v16, *, tq, tk, pos0):
    B, _, S, _ = q16.shape
    n_q = S // tq
    sched = _schedule(n_q, tq, tk, pos0, lambda q_first, k_last: k_last < q_first)
    steps = int(sched[0].shape[0])
    qspec = pl.BlockSpec((None, HEAD_GROUPS, tq, LANES), lambda b, s, qi, kb, f, l, m: (b, 0, qi[s], 0))
    kspec = pl.BlockSpec((None, HEAD_GROUPS, tk, LANES), lambda b, s, qi, kb, f, l, m: (b, 0, kb[s], 0))
    grid_spec = pltpu.PrefetchScalarGridSpec(
        num_scalar_prefetch=5,
        grid=(B, steps),
        in_specs=[qspec, kspec, kspec,
                  pl.BlockSpec((2 * tk, tk), lambda b, s, qi, kb, f, l, m: (0, 0))],
        out_specs=pl.BlockSpec((tq, SB_W), lambda b, s, qi, kb, f, l, m: (b * n_q + qi[s], 0)),
        scratch_shapes=[pltpu.VMEM((HEAD_GROUPS, tq, LANES), F32),
                        pltpu.VMEM((SB_HEADS, tq, LANES), F32)],
    )
    return pl.pallas_call(
        functools.partial(_sb_kernel, tq=tq, tk=tk, pos0=pos0),
        grid_spec=grid_spec,
        out_shape=jax.ShapeDtypeStruct((B * S, SB_W), BF16),
        compiler_params=_params(("parallel", "arbitrary")),
        name="stick_breaking_attention",
    )(*sched, q16, k16, v16, _tri_ones_stacked(tk))


def _df_kernel(qi_ref, kb_ref, first_ref, last_ref, masked_ref, q_ref, k_ref, v_ref, lam_ref, g_ref,
               o_ref, m_ref, l_ref, acc_ref, *, tq, tk, pos0, lam_init):
    s = pl.program_id(1)
    qi = qi_ref[s]
    kb = kb_ref[s]

    @pl.when(first_ref[s] == 1)
    def _():
        m_ref[...] = jnp.full(m_ref.shape, NEG_BIG, F32)
        l_ref[...] = jnp.zeros(l_ref.shape, F32)
        acc_ref[...] = jnp.zeros(acc_ref.shape, F32)

    reps = tk // LANES

    def tile(use_mask):
        half = _half_masks()
        if use_mask:
            q_pos, k_pos = _positions(qi, kb, tq, tk, pos0)
            mask = k_pos <= (q_pos | (CHUNK - 1))

        def head(h, c):
            q = q_ref[h]
            k = k_ref[h]
            v = v_ref[h]
            for j in range(2):
                idx = 2 * h + j
                sc = _dot_nt(q * half[j], k)
                if use_mask:
                    sc = jnp.where(mask, sc, NEG_BIG)
                m_old = m_ref[idx]
                m_new = jnp.maximum(m_old, jnp.max(sc, axis=1, keepdims=True))
                alpha = jnp.exp(m_old - m_new)
                p = jnp.exp(sc - jnp.concatenate([m_new] * reps, axis=1))
                l_ref[idx] = alpha * l_ref[idx] + jnp.sum(p, axis=1, keepdims=True)
                acc_ref[idx] = alpha * acc_ref[idx] + _dot(p.astype(BF16), v)
                m_ref[idx] = m_new
            return c

        lax.fori_loop(0, DF_HEADS, head, 0, unroll=True)

    @pl.when(masked_ref[s] == 1)
    def _():
        tile(True)

    @pl.when(masked_ref[s] == 0)
    def _():
        tile(False)

    @pl.when(last_ref[s] == 1)
    def _():
        lv = lam_ref[...]
        lam = (jnp.exp(jnp.sum(lv[0:1] * lv[1:2], axis=1, keepdims=True))
               - jnp.exp(jnp.sum(lv[2:3] * lv[3:4], axis=1, keepdims=True)) + lam_init)
        gain = g_ref[...] * (1.0 - lam_init)
        for h in range(DF_HEADS):
            o = acc_ref[2 * h] / l_ref[2 * h] - lam * (acc_ref[2 * h + 1] / l_ref[2 * h + 1])
            o = o * lax.rsqrt(jnp.mean(o * o, axis=-1, keepdims=True) + RMS_EPS)
            o_ref[:, h * LANES:(h + 1) * LANES] = (o * gain).astype(BF16)


def _df_attention(q16, k16, v16, lam_vecs, subln_g, *, tq, tk, pos0, lam_init):
    B, _, S, _ = q16.shape
    n_q = S // tq
    sched = _schedule(n_q, tq, tk, pos0, lambda q_first, k_last: k_last <= (q_first | (CHUNK - 1)))
    steps = int(sched[0].shape[0])
    qspec = pl.BlockSpec((None, DF_HEADS, tq, LANES), lambda b, s, qi, kb, f, l, m: (b, 0, qi[s], 0))
    kspec = pl.BlockSpec((None, DF_HEADS, tk, LANES), lambda b, s, qi, kb, f, l, m: (b, 0, kb[s], 0))
    const = lambda b, s, qi, kb, f, l, m: (0, 0)
    grid_spec = pltpu.PrefetchScalarGridSpec(
        num_scalar_prefetch=5,
        grid=(B, steps),
        in_specs=[qspec, kspec, kspec,
                  pl.BlockSpec((4, DF_DIM), const), pl.BlockSpec((1, LANES), const)],
        out_specs=pl.BlockSpec((tq, DF_W), lambda b, s, qi, kb, f, l, m: (b * n_q + qi[s], 0)),
        scratch_shapes=[pltpu.VMEM((2 * DF_HEADS, tq, LANES), F32),
                        pltpu.VMEM((2 * DF_HEADS, tq, LANES), F32),
                        pltpu.VMEM((2 * DF_HEADS, tq, LANES), F32)],
    )
    return pl.pallas_call(
        functools.partial(_df_kernel, tq=tq, tk=tk, pos0=pos0, lam_init=lam_init),
        grid_spec=grid_spec,
        out_shape=jax.ShapeDtypeStruct((B * S, DF_W), BF16),
        compiler_params=_params(("parallel", "arbitrary")),
        name="differential_attention",
    )(*sched, q16, k16, v16, lam_vecs, subln_g.reshape(1, LANES))


def _merge_kernel(x_ref, oa_ref, ob_ref, sga_ref, sgb_ref, wa_ref, wb_ref, wo_ref, g_ref, b_ref, o_ref):
    ya = _dot(oa_ref[...], wa_ref[...])
    yb = _dot(ob_ref[...], wb_ref[...])
    mix = (sga_ref[...] * ya + sgb_ref[...] * yb).astype(BF16)
    h = DN_ALPHA * x_ref[...] + _dot(mix, wo_ref[...])
    o_ref[...] = _layer_norm(h, g_ref[...], b_ref[...])


def _merge(xf, oa, ob, sga, sgb, wa16, wb16, wo16, g, b):
    N, D = xf.shape
    tm = 256
    row = lambda i: (i, 0)
    const = lambda i: (0, 0)
    return pl.pallas_call(
        _merge_kernel,
        grid=(N // tm,),
        in_specs=[pl.BlockSpec((tm, D), row), pl.BlockSpec((tm, SB_W), row), pl.BlockSpec((tm, DF_W), row),
                  pl.BlockSpec((tm, D), row), pl.BlockSpec((tm, D), row),
                  pl.BlockSpec((SB_W, D), const), pl.BlockSpec((DF_W, D), const), pl.BlockSpec((D, D), const),
                  pl.BlockSpec((1, D), const), pl.BlockSpec((1, D), const)],
        out_specs=pl.BlockSpec((tm, D), row),
        out_shape=jax.ShapeDtypeStruct((N, D), F32),
        compiler_params=_params(("parallel",)),
        name="merge_ln1",
    )(xf, oa, ob, sga, sgb, wa16, wb16, wo16, g.reshape(1, D), b.reshape(1, D))


def _proj_kernel(x_ref, w_ref, o_ref):
    o_ref[...] = _dot(x_ref[...].astype(BF16), w_ref[...])


def _project(xf, w16):
    N, D = xf.shape
    tm = 256
    return pl.pallas_call(
        _proj_kernel,
        grid=(N // tm,),
        in_specs=[pl.BlockSpec((tm, D), lambda i: (i, 0)), pl.BlockSpec(w16.shape, lambda i: (0, 0))],
        out_specs=pl.BlockSpec((tm, w16.shape[1]), lambda i: (i, 0)),
        out_shape=jax.ShapeDtypeStruct((N, w16.shape[1]), F32),
        compiler_params=_params(("parallel",)),
        name="memory_projection",
    )(xf, w16)


def _memattn_kernel(x_ref, mk_ref, mv_ref, wq_ref, wo_ref, g_ref, b_ref, *rest):
    o_ref = rest[-1]
    x = x_ref[...]
    q = (_dot(x.astype(BF16), wq_ref[...]) * (MEM_DIM ** -0.5)).astype(BF16)
    outs = []
    for h in range(MEM_HEADS):
        sl = slice(h * MEM_DIM, (h + 1) * MEM_DIM)
        sc = _dot_nt(q[:, sl], mk_ref[:, sl].astype(BF16))
        sc = sc - jnp.max(sc, axis=-1, keepdims=True)
        p = jnp.exp(sc)
        p = p / jnp.sum(p, axis=-1, keepdims=True)
        outs.append(_dot(p.astype(BF16), mv_ref[:, sl].astype(BF16)).astype(BF16))
    o = jnp.concatenate(outs, axis=1)
    h2 = DN_ALPHA * x + _dot(o, wo_ref[...])
    o_ref[...] = _layer_norm(h2, g_ref[...], b_ref[...])


def _memory_attention(x1, mem_k, mem_v, wq16, wo16, g, b, *, batch, seq, out_rows, row_offset, prev=None):
    N, D = x1.shape
    M = mem_k.shape[1]
    tm = min(256, seq)
    spb = seq // tm
    off = row_offset // tm
    const = lambda bi, i: (0, 0)
    in_specs = [pl.BlockSpec((tm, D), lambda bi, i: (bi * spb + i, 0)),
                pl.BlockSpec((None, M, D), lambda bi, i: (bi, 0, 0)),
                pl.BlockSpec((None, M, D), lambda bi, i: (bi, 0, 0)),
                pl.BlockSpec((D, D), const), pl.BlockSpec((D, D), const),
                pl.BlockSpec((1, D), const), pl.BlockSpec((1, D), const)]
    args = [x1, mem_k, mem_v, wq16, wo16, g.reshape(1, D), b.reshape(1, D)]
    aliases = {}
    if prev is not None:
        in_specs.append(pl.BlockSpec(memory_space=pl.ANY))
        args.append(prev)
        aliases = {len(args) - 1: 0}
    return pl.pallas_call(
        _memattn_kernel,
        grid=(batch, spb),
        in_specs=in_specs,
        out_specs=pl.BlockSpec((tm, D), lambda bi, i: (off + bi * spb + i, 0)),
        out_shape=jax.ShapeDtypeStruct((out_rows, D), F32),
        input_output_aliases=aliases,
        compiler_params=_params(("parallel", "arbitrary")),
        name="memory_attention_ln2",
    )(*args)


def _router_kernel(x_ref, w_ref, b_ref, tri_ref, o_ref, cnt_ref, run_ref):
    @pl.when(pl.program_id(0) == 0)
    def _():
        run_ref[...] = jnp.zeros(run_ref.shape, F32)

    x = x_ref[...]
    tm = x.shape[0]
    x_hi = x.astype(BF16)
    x_lo = (x - x_hi.astype(F32)).astype(BF16)
    w = w_ref[...]
    w_hi = w.astype(BF16)
    w_lo = (w - w_hi.astype(F32)).astype(BF16)
    logits = _dot(x_hi, w_hi) + _dot(x_lo, w_hi) + _dot(x_hi, w_lo) + b_ref[...]
    lane = lax.broadcasted_iota(jnp.int32, (tm, LANES), 1)
    big = jnp.int32(LANES)
    glog = jnp.where(lane < N_GROUPS, logits, -jnp.inf)
    gmax = jnp.max(glog, axis=-1, keepdims=True)
    grp = jnp.min(jnp.where(glog == gmax, lane, big), axis=-1, keepdims=True)
    g_prob = 1.0 / jnp.sum(jnp.exp(glog - gmax), axis=-1, keepdims=True)
    lo = N_GROUPS + grp * EXPERTS_PER_GROUP
    in_grp = (lane >= lo) & (lane < lo + EXPERTS_PER_GROUP)
    el = jnp.where(in_grp, logits, -jnp.inf)
    v1 = jnp.max(el, axis=-1, keepdims=True)
    i1 = jnp.min(jnp.where(el == v1, lane, big), axis=-1, keepdims=True)
    el2 = jnp.where(lane == i1, -jnp.inf, el)
    v2 = jnp.max(el2, axis=-1, keepdims=True)
    i2 = jnp.min(jnp.where(el2 == v2, lane, big), axis=-1, keepdims=True)
    e2 = jnp.exp(v2 - v1)
    den = 1.0 / (1.0 + e2)
    g1 = den * g_prob
    g2 = e2 * den * g_prob
    hit1 = lane == i1
    hit2 = lane == i2
    member = jnp.where(hit1 | hit2, 1.0, 0.0)
    before = _dot(tri_ref[...], member.astype(BF16)) + run_ref[...]
    r1 = jnp.sum(jnp.where(hit1, before, 0.0), axis=-1, keepdims=True)
    r2 = jnp.sum(jnp.where(hit2, before, 0.0), axis=-1, keepdims=True)
    run = run_ref[...] + jnp.sum(member, axis=0, keepdims=True)
    run_ref[...] = run
    cnt_ref[...] = run
    out = jnp.where(lane == 0, (i1 - N_GROUPS).astype(F32), 0.0)
    out = jnp.where(lane == 1, (i2 - N_GROUPS).astype(F32), out)
    out = jnp.where(lane == 2, g1, out)
    out = jnp.where(lane == 3, g2, out)
    out = jnp.where(lane == 4, r1, out)
    out = jnp.where(lane == 5, r2, out)
    o_ref[...] = out


def _router(x2, w_group, b_group, w_router, b_router):
    N, D = x2.shape
    tm = 256
    w = jnp.zeros((D, LANES), F32).at[:, :N_GROUPS].set(w_group).at[:, N_GROUPS:N_GROUPS + N_EXPERTS].set(w_router)
    b = jnp.zeros((1, LANES), F32).at[0, :N_GROUPS].set(b_group).at[0, N_GROUPS:N_GROUPS + N_EXPERTS].set(b_router)
    r_idx = np.arange(tm)
    strict_lower = jnp.asarray((r_idx[:, None] > r_idx[None, :]).astype(np.float32)).astype(BF16)
    r, cnt = pl.pallas_call(
        _router_kernel,
        grid=(N // tm,),
        in_specs=[pl.BlockSpec((tm, D), lambda i: (i, 0)), pl.BlockSpec((D, LANES), lambda i: (0, 0)),
                  pl.BlockSpec((1, LANES), lambda i: (0, 0)), pl.BlockSpec((tm, tm), lambda i: (0, 0))],
        out_specs=[pl.BlockSpec((tm, LANES), lambda i: (i, 0)), pl.BlockSpec((1, LANES), lambda i: (0, 0))],
        out_shape=[jax.ShapeDtypeStruct((N, LANES), F32), jax.ShapeDtypeStruct((1, LANES), F32)],
        scratch_shapes=[pltpu.VMEM((1, LANES), F32)],
        compiler_params=_params(("arbitrary",)),
        name="router",
    )(x2, w, b, strict_lower)
    expert = r[:, 0:2].astype(jnp.int32)
    gate = r[:, 2:4]
    rank = r[:, 4:6].astype(jnp.int32)
    counts = cnt[0, N_GROUPS:N_GROUPS + N_EXPERTS].astype(jnp.int32)
    return expert, gate, rank, counts


def _row_scatter(idx_ref, k, src_ref, dst_hbm, sem, rows):
    for r in range(rows):
        p = idx_ref[0, k, r]
        pltpu.make_async_copy(src_ref.at[pl.ds(r, 1)], dst_hbm.at[pl.ds(p, 1)], sem).start(priority=r % 2)


def _dispatch_kernel(pos_ref, x_ref, xs_in_hbm, xs_hbm, buf_ref, sem_ref, *, tm, n_tiles):
    del xs_in_hbm
    i = pl.program_id(0)
    slot = i % 2

    def wait(sl):
        for k in range(TOP_K):
            pltpu.make_async_copy(buf_ref.at[sl], xs_hbm.at[pl.ds(0, tm)], sem_ref.at[sl]).wait()

    @pl.when(i >= 2)
    def _():
        wait(slot)

    buf_ref[slot] = x_ref[...]
    for k in range(TOP_K):
        _row_scatter(pos_ref, k, buf_ref.at[slot], xs_hbm, sem_ref.at[slot], tm)

    @pl.when(i == n_tiles - 1)
    def _():
        wait(slot)
        if n_tiles > 1:
            wait(1 - slot)


def _dispatch(x2, pos3, rows):
    N, D = x2.shape
    nt, _, tm = pos3.shape
    xs0 = jnp.zeros((rows, D), F32)
    return pl.pallas_call(
        functools.partial(_dispatch_kernel, tm=tm, n_tiles=nt),
        grid=(nt,),
        in_specs=[pl.BlockSpec((1, TOP_K, tm), lambda i: (i, 0, 0), memory_space=pltpu.SMEM),
                  pl.BlockSpec((tm, D), lambda i: (i, 0)),
                  pl.BlockSpec(memory_space=pl.ANY)],
        out_specs=pl.BlockSpec(memory_space=pl.ANY),
        out_shape=jax.ShapeDtypeStruct((rows, D), F32),
        scratch_shapes=[pltpu.VMEM((2, tm, D), F32), pltpu.SemaphoreType.DMA((2,))],
        input_output_aliases={2: 0},
        compiler_params=_params(("arbitrary",)),
        name="expert_dispatch",
    )(pos3, x2, xs0)


def _row_gather(idx_ref, k, src_hbm, dst_ref, sem, rows):
    for r in range(rows):
        t = idx_ref[0, k, r]
        pltpu.make_async_copy(src_hbm.at[pl.ds(t, 1)], dst_ref.at[pl.ds(r, 1)], sem).start(priority=r % 2)


def _gather_wait(src_hbm, dst_ref, sem, rows):
    pltpu.make_async_copy(src_hbm.at[pl.ds(0, rows)], dst_ref, sem).wait()


def _expert_kernel(be_ref, used_ref, x_ref, wg_ref, wu_ref, wd_ref, o_ref):
    i = pl.program_id(0)

    @pl.when(i < used_ref[0])
    def _():
        xb = x_ref[...].astype(BF16)
        hg = _dot(xb, wg_ref[...])
        hu = _dot(xb, wu_ref[...])
        hid = (hg * jax.nn.sigmoid(hg) * hu).astype(BF16)
        o_ref[...] = _dot(hid, wd_ref[...])

    @pl.when(i >= used_ref[0])
    def _():
        o_ref[...] = jnp.zeros(o_ref.shape, F32)


def _experts(xs, blk_e, n_used, wg16, wu16, wd16):
    R, D = xs.shape
    nb = R // EBLOCK
    grid_spec = pltpu.PrefetchScalarGridSpec(
        num_scalar_prefetch=2,
        grid=(nb,),
        in_specs=[pl.BlockSpec((EBLOCK, D), lambda i, be, nu: (i, 0)),
                  pl.BlockSpec((None, D, D_EXPERT), lambda i, be, nu: (be[i], 0, 0)),
                  pl.BlockSpec((None, D, D_EXPERT), lambda i, be, nu: (be[i], 0, 0)),
                  pl.BlockSpec((None, D_EXPERT, D), lambda i, be, nu: (be[i], 0, 0))],
        out_specs=pl.BlockSpec((EBLOCK, D), lambda i, be, nu: (i, 0)),
    )
    return pl.pallas_call(
        _expert_kernel,
        grid_spec=grid_spec,
        out_shape=jax.ShapeDtypeStruct((R, D), F32),
        compiler_params=_params(("arbitrary",)),
        name="grouped_experts",
    )(blk_e, n_used, xs, wg16, wu16, wd16)


def _combine_kernel(pos_cur_ref, pos_next_ref, x_ref, gate_ref, g_ref, b_ref, y_hbm, oa_ref, ob_ref,
                    buf_ref, sem_ref, *, tm, n_tiles, n_tiles_a):
    i = pl.program_id(0)
    slot = i % 2

    def gather(pos_ref, sl):
        for k in range(TOP_K):
            _row_gather(pos_ref, k, y_hbm, buf_ref.at[sl, k], sem_ref.at[sl], tm)

    def wait(sl):
        for k in range(TOP_K):
            _gather_wait(y_hbm, buf_ref.at[sl, k], sem_ref.at[sl], tm)

    @pl.when(i == 0)
    def _():
        gather(pos_cur_ref, 0)

    wait(slot)
    gather(pos_next_ref, 1 - slot)
    gate = gate_ref[...]
    moe = buf_ref[slot, 0] * gate[:, 0:1] + buf_ref[slot, 1] * gate[:, 1:2]
    h = DN_ALPHA * x_ref[...] + moe
    res = _layer_norm(h, g_ref[...], b_ref[...])

    @pl.when(i < n_tiles_a)
    def _():
        oa_ref[...] = res

    @pl.when(i >= n_tiles_a)
    def _():
        ob_ref[...] = res

    @pl.when(i == n_tiles - 1)
    def _():
        wait(1 - slot)


def _combine(x2, ys, pos3, gate, g, b, rows_a):
    N, D = x2.shape
    nt, _, tm = pos3.shape
    nta = rows_a // tm
    gate_p = jnp.zeros((N, LANES), F32).at[:, 0:TOP_K].set(gate)
    return pl.pallas_call(
        functools.partial(_combine_kernel, tm=tm, n_tiles=nt, n_tiles_a=nta),
        grid=(nt,),
        in_specs=[pl.BlockSpec((1, TOP_K, tm), lambda i: (i, 0, 0), memory_space=pltpu.SMEM),
                  pl.BlockSpec((1, TOP_K, tm), lambda i: (jnp.minimum(i + 1, nt - 1), 0, 0),
                               memory_space=pltpu.SMEM),
                  pl.BlockSpec((tm, D), lambda i: (i, 0)),
                  pl.BlockSpec((tm, LANES), lambda i: (i, 0)),
                  pl.BlockSpec((1, D), lambda i: (0, 0)), pl.BlockSpec((1, D), lambda i: (0, 0)),
                  pl.BlockSpec(memory_space=pl.ANY)],
        out_specs=[pl.BlockSpec((tm, D), lambda i: (jnp.minimum(i, nta - 1), 0)),
                   pl.BlockSpec((tm, D), lambda i: (jnp.maximum(i - nta, 0), 0))],
        out_shape=[jax.ShapeDtypeStruct((rows_a, D), F32), jax.ShapeDtypeStruct((N - rows_a, D), F32)],
        scratch_shapes=[pltpu.VMEM((2, TOP_K, tm, D), F32), pltpu.SemaphoreType.DMA((2,))],
        compiler_params=_params(("arbitrary",)),
        name="combine_ln3",
    )(pos3, pos3, x2, gate_p, g.reshape(1, D), b.reshape(1, D), ys)


def _dispatch_plan(expert, rank, counts):
    N, K = expert.shape
    A = N * K
    E = N_EXPERTS
    padded = (counts + EBLOCK - 1) // EBLOCK * EBLOCK
    pad_end = jnp.cumsum(padded)
    pad_start = pad_end - padded
    n_blocks = (A + E * (EBLOCK - 1) + EBLOCK - 1) // EBLOCK
    pos = (pad_start[expert] + rank).astype(jnp.int32)
    blk_e = jnp.minimum(jnp.searchsorted(pad_end, jnp.arange(n_blocks) * EBLOCK, side='right'), E - 1)
    n_used = (pad_end[-1] // EBLOCK).astype(jnp.int32).reshape(1)
    return pos, blk_e.astype(jnp.int32), n_used, n_blocks * EBLOCK


def _hier_moe_ln3(x2, rows_a, w_group, b_group, w_router, b_router, wg16, wu16, wd16, g, b):
    N = x2.shape[0]
    tm = 256
    expert, gate, rank, counts = _router(x2, w_group, b_group, w_router, b_router)
    pos, blk_e, n_used, rows = _dispatch_plan(expert, rank, counts)
    pos3 = pos.reshape(N // tm, tm, TOP_K).transpose(0, 2, 1)
    xs = _dispatch(x2, pos3, rows)
    ys = _experts(xs, blk_e, n_used, wg16, wu16, wd16)
    return _combine(x2, ys, pos3, gate, g, b, rows_a)


def _mixers(x, caches, lam_init, w):
    B, S, D = x.shape
    N = B * S
    xf = x.reshape(N, D)
    past = 0 if caches is None else caches[0].shape[1]
    (ka, va, kb, vb, sga, sgb, qa16, ka16, va16, qb16, kb16, vb16) = _in_projection(x, w['w_in'], past)
    if caches is None:
        tq = min(512, S)
        tk = min(256, S)
        keys = (ka16, va16, kb16, vb16)
    else:
        tq = S
        tk = 256
        keys = tuple(_pack_keys(c.reshape(B, past, -1), n, past + tk)
                     for c, n in zip(caches, (ka16, va16, kb16, vb16)))
    oa = _sb_attention(qa16, keys[0], keys[1], tq=tq, tk=tk, pos0=past)
    tk_df = tk if caches is not None else min(1024, S)
    ob = _df_attention(qb16, keys[2], keys[3], w['lam'], w['subln_g'], tq=tq, tk=tk_df, pos0=past,
                       lam_init=lam_init)
    x1 = _merge(xf, oa, ob, sga, sgb, w['w_branch_a'], w['w_branch_b'], w['w_out'], w['ln1_g'], w['ln1_b'])
    new_rows = (ka.reshape(B, S, SB_HEADS, SB_DIM), va.reshape(B, S, SB_HEADS, SB_DIM),
                kb.reshape(B, S, DF_HEADS, 2 * DF_DIM), vb.reshape(B, S, DF_HEADS, 2 * DF_DIM))
    return x1, new_rows


def kernel(x_prompt, mem_prompt, x_sample, cache_sb_k, cache_sb_v, cache_diff_k, cache_diff_v, cache_mem_k, cache_mem_v, w_in, lam_q1, lam_k1, lam_q2, lam_k2, subln_g, w_branch_a, w_branch_b, w_out, ln1_g, ln1_b, w_mq, w_mk, w_mv, w_mo, ln2_g, ln2_b, w_group, b_group, w_router, b_router, w_up, w_gate, w_down, ln3_g, ln3_b):
    depth = w_in.shape[0]
    yp, ys = x_prompt, x_sample
    Bp, n_mem = mem_prompt.shape[0], mem_prompt.shape[1]
    Bs = x_sample.shape[0]
    outs = [[] for _ in range(10)]
    for l in range(depth):
        lam_init = 0.8 - 0.6 * math.exp(-0.3 * l)
        w = {
            'w_in': w_in[l].astype(BF16),
            'lam': jnp.stack([lam_q1[l], lam_k1[l], lam_q2[l], lam_k2[l]]).astype(F32),
            'subln_g': subln_g[l],
            'w_branch_a': w_branch_a[l].astype(BF16), 'w_branch_b': w_branch_b[l].astype(BF16),
            'w_out': w_out[l].astype(BF16), 'ln1_g': ln1_g[l], 'ln1_b': ln1_b[l],
            'w_mq': w_mq[l].astype(BF16), 'w_mo': w_mo[l].astype(BF16),
            'ln2_g': ln2_g[l], 'ln2_b': ln2_b[l],
            'w_group': w_group[l], 'b_group': b_group[l], 'w_router': w_router[l], 'b_router': b_router[l],
            'w_gate': w_gate[l].astype(BF16), 'w_up': w_up[l].astype(BF16), 'w_down': w_down[l].astype(BF16),
            'ln3_g': ln3_g[l], 'ln3_b': ln3_b[l],
        }
        memf = mem_prompt.reshape(Bp * n_mem, D_MODEL)
        mk = _project(memf, w_mk[l].astype(BF16)).reshape(Bp, n_mem, D_MODEL)
        mv = _project(memf, w_mv[l].astype(BF16)).reshape(Bp, n_mem, D_MODEL)
        Np = Bp * yp.shape[1]
        Ns = Bs * ys.shape[1]
        x1p, rows_p = _mixers(yp, None, lam_init, w)
        for lst, val in zip(outs[0:6], rows_p + (mk.reshape(Bp, n_mem, MEM_HEADS, MEM_DIM),
                                               mv.reshape(Bp, n_mem, MEM_HEADS, MEM_DIM))):
            lst.append(val)
        caches = (cache_sb_k[l], cache_sb_v[l], cache_diff_k[l], cache_diff_v[l])
        x1s, rows_s = _mixers(ys, caches, lam_init, w)
        for lst, val in zip(outs[6:10], rows_s):
            lst.append(val)
        x2 = _memory_attention(x1p, mk, mv, w['w_mq'], w['w_mo'], w['ln2_g'], w['ln2_b'],
                               batch=Bp, seq=yp.shape[1], out_rows=Np + Ns, row_offset=0,
                               prev=jnp.zeros((Np + Ns, D_MODEL), F32))
        x2 = _memory_attention(x1s, cache_mem_k[l].reshape(Bs, n_mem, D_MODEL),
                               cache_mem_v[l].reshape(Bs, n_mem, D_MODEL),
                               w['w_mq'], w['w_mo'], w['ln2_g'], w['ln2_b'],
                               batch=Bs, seq=ys.shape[1], out_rows=Np + Ns, row_offset=Np, prev=x2)
        x3p, x3s = _hier_moe_ln3(x2, Np, w['w_group'], w['b_group'], w['w_router'], w['b_router'],
                                 w['w_gate'], w['w_up'], w['w_down'], w['ln3_g'], w['ln3_b'])
        yp = x3p.reshape(yp.shape)
        ys = x3s.reshape(ys.shape)
    return (yp, ys) + tuple(jnp.stack(o) for o in outs)
```

```python
import functools
import math

import jax
import jax.numpy as jnp
import numpy as np
from jax import lax
from jax.experimental import pallas as pl
from jax.experimental.pallas import tpu as pltpu

F32 = jnp.float32
BF16 = jnp.bfloat16

D_MODEL = 1024
SB_HEADS = 8
SB_DIM = 64
SB_W = SB_HEADS * SB_DIM
DF_HEADS = 4
DF_DIM = 64
DF_W = DF_HEADS * 2 * DF_DIM
CHUNK = 64
MEM_HEADS = 4
MEM_DIM = D_MODEL // MEM_HEADS
N_GROUPS = 4
EXPERTS_PER_GROUP = 8
N_EXPERTS = N_GROUPS * EXPERTS_PER_GROUP
TOP_K = 2
D_EXPERT = 512
EBLOCK = 256
ROPE_THETA = 10000.0
LN_EPS = 1e-5
RMS_EPS = 1e-5
DEPTH = 1
DN_ALPHA = (2 * DEPTH) ** 0.25
IN_W = 3 * SB_W + 3 * DF_W + 2 * D_MODEL

LANES = 128
HEAD_GROUPS = 4
NEG_BIG = -1e30
VMEM_LIMIT = 48 * 1024 * 1024


def _dot(a, b):
    return jnp.dot(a, b, preferred_element_type=F32)


def _dot_nt(a, b):
    return lax.dot_general(a, b, (((1,), (1,)), ((), ())), preferred_element_type=F32)


def _layer_norm(h, g, b):
    mu = jnp.mean(h, axis=-1, keepdims=True)
    d = h - mu
    var = jnp.mean(d * d, axis=-1, keepdims=True)
    return d * lax.rsqrt(var + LN_EPS) * g + b


def _params(sem, vmem=VMEM_LIMIT):
    return pltpu.CompilerParams(dimension_semantics=sem, vmem_limit_bytes=vmem)


def _inproj_kernel(x_ref, w_ref, cos_ref, sina_ref, sinb_ref,
                   ka_ref, va_ref, kb_ref, vb_ref, sga_ref, sgb_ref,
                   qa16_ref, ka16_ref, va16_ref, qb16_ref, kb16_ref, vb16_ref):
    xb = x_ref[...].astype(BF16)

    def mm(off, width):
        return _dot(xb, w_ref[:, off:off + width])

    def put_groups(ref, val):
        for g in range(HEAD_GROUPS):
            ref[g] = val[:, g * LANES:(g + 1) * LANES].astype(BF16)

    def rope(t):
        n = t.shape[1]
        reps = n // LANES
        cos = jnp.concatenate([cos_ref[...]] * reps, axis=1)
        sina = jnp.concatenate([sina_ref[...]] * reps, axis=1)
        sinb = jnp.concatenate([sinb_ref[...]] * reps, axis=1)
        fwd = pltpu.roll(t, n - DF_DIM // 2, 1)
        bwd = pltpu.roll(t, DF_DIM // 2, 1)
        return t * cos + fwd * sina + bwd * sinb

    qa = mm(0, SB_W)
    put_groups(qa16_ref, qa * (SB_DIM ** -0.5))
    ka = mm(SB_W, SB_W)
    ka_ref[...] = ka
    put_groups(ka16_ref, ka)
    va = mm(2 * SB_W, SB_W)
    va_ref[...] = va
    put_groups(va16_ref, va)
    off = 3 * SB_W
    qb = rope(mm(off, DF_W))
    put_groups(qb16_ref, qb * (DF_DIM ** -0.5))
    kb = rope(mm(off + DF_W, DF_W))
    kb_ref[...] = kb
    put_groups(kb16_ref, kb)
    vb = mm(off + 2 * DF_W, DF_W)
    vb_ref[...] = vb
    put_groups(vb16_ref, vb)
    off = off + 3 * DF_W
    sga_ref[...] = jax.nn.sigmoid(mm(off, D_MODEL))
    sgb_ref[...] = jax.nn.sigmoid(mm(off + D_MODEL, D_MODEL))


def _rope_tables(pos0, seq):
    half = DF_DIM // 2
    inv_freq = jnp.exp(jnp.arange(half, dtype=F32) * (-math.log(ROPE_THETA) / half))
    ang = (pos0 + jnp.arange(seq, dtype=jnp.int32)).astype(F32)[:, None] * inv_freq[None, :]
    cos, sin = jnp.cos(ang), jnp.sin(ang)
    zero = jnp.zeros_like(sin)
    reps = LANES // DF_DIM
    cos_t = jnp.tile(jnp.concatenate([cos, cos], axis=1), (1, reps))
    sina_t = jnp.tile(jnp.concatenate([-sin, zero], axis=1), (1, reps))
    sinb_t = jnp.tile(jnp.concatenate([zero, sin], axis=1), (1, reps))
    return cos_t, sina_t, sinb_t


def _in_projection(x, w_in16, pos0):
    B, S, D = x.shape
    N = B * S
    tm = min(256, S)
    spb = S // tm
    xf = x.reshape(N, D)
    cos_t, sina_t, sinb_t = _rope_tables(pos0, S)
    row = lambda i: (i, 0)
    tab = lambda i: (i % spb, 0)
    grp = lambda i: (i // spb, 0, i % spb, 0)
    f32_512 = jax.ShapeDtypeStruct((N, SB_W), F32)
    f32_1024 = jax.ShapeDtypeStruct((N, D), F32)
    g16 = jax.ShapeDtypeStruct((B, HEAD_GROUPS, S, LANES), BF16)
    spec512 = pl.BlockSpec((tm, SB_W), row)
    spec1024 = pl.BlockSpec((tm, D), row)
    gspec = pl.BlockSpec((None, HEAD_GROUPS, tm, LANES), grp)
    tspec = pl.BlockSpec((tm, LANES), tab)
    return pl.pallas_call(
        _inproj_kernel,
        grid=(N // tm,),
        in_specs=[spec1024, pl.BlockSpec((D, IN_W), lambda i: (0, 0)), tspec, tspec, tspec],
        out_specs=[spec512, spec512, spec512, spec512, spec1024, spec1024,
                   gspec, gspec, gspec, gspec, gspec, gspec],
        out_shape=[f32_512, f32_512, f32_512, f32_512, f32_1024, f32_1024,
                   g16, g16, g16, g16, g16, g16],
        compiler_params=_params(("parallel",), 56 * 1024 * 1024),
        name="in_projection",
    )(xf, w_in16, cos_t, sina_t, sinb_t)


def _pack_kernel(c_ref, n_ref, o_ref, *, past, new_rows):
    c = c_ref[...]
    for g in range(HEAD_GROUPS):
        o_ref[g, 0:past, :] = c[:, g * LANES:(g + 1) * LANES].astype(BF16)
    pad = o_ref.shape[1] - past - new_rows
    o_ref[:, past:past + new_rows, :] = n_ref[...]
    o_ref[:, past + new_rows:, :] = jnp.zeros((HEAD_GROUPS, pad, LANES), BF16)


def _pack_keys(cache, new16, total):
    B, P, W = cache.shape
    S = new16.shape[2]
    return pl.pallas_call(
        functools.partial(_pack_kernel, past=P, new_rows=S),
        grid=(B,),
        in_specs=[pl.BlockSpec((None, P, W), lambda b: (b, 0, 0)),
                  pl.BlockSpec((None, HEAD_GROUPS, S, LANES), lambda b: (b, 0, 0, 0))],
        out_specs=pl.BlockSpec((None, HEAD_GROUPS, total, LANES), lambda b: (b, 0, 0, 0)),
        out_shape=jax.ShapeDtypeStruct((B, HEAD_GROUPS, total, LANES), BF16),
        compiler_params=_params(("parallel",)),
        name="pack_keys",
    )(cache, new16)


def _schedule(n_q, tq, tk, pos0, fully_visible):
    qi, kb, first, last, masked = [], [], [], [], []
    for i in range(n_q):
        top = -(-(pos0 + (i + 1) * tq) // tk) - 1
        for j in range(top, -1, -1):
            qi.append(i)
            kb.append(j)
            first.append(1 if j == top else 0)
            last.append(1 if j == 0 else 0)
            masked.append(0 if fully_visible(pos0 + i * tq, (j + 1) * tk - 1) else 1)
    mk = lambda v: jnp.asarray(np.asarray(v, np.int32))
    return mk(qi), mk(kb), mk(first), mk(last), mk(masked)


def _half_masks():
    lane = lax.broadcasted_iota(jnp.int32, (1, LANES), 1)
    lo = jnp.where(lane < LANES // 2, 1.0, 0.0).astype(BF16)
    hi = jnp.where(lane >= LANES // 2, 1.0, 0.0).astype(BF16)
    return lo, hi


def _positions(qi, kb, tq, tk, pos0):
    q_pos = pos0 + qi * tq + lax.broadcasted_iota(jnp.int32, (tq, 1), 0)
    k_pos = kb * tk + lax.broadcasted_iota(jnp.int32, (1, tk), 1)
    return q_pos, k_pos


LOG2E = 1.4426950408889634
SOFTPLUS_CLAMP = 64.0
STACKED_ROWS_MAX = 512
BF16_BITS = 0xFFFF0000


def _sb_kernel(qi_ref, kb_ref, first_ref, last_ref, masked_ref, q_ref, k_ref, v_ref, tri_ref, o_ref,
               acc_ref, carry_ref, *, tq, tk, pos0):
    stack_heads = SB_HEADS * tq <= STACKED_ROWS_MAX
    s = pl.program_id(1)
    qi = qi_ref[s]
    kb = kb_ref[s]

    @pl.when(first_ref[s] == 1)
    def _():
        acc_ref[...] = jnp.zeros(acc_ref.shape, F32)
        carry_ref[...] = jnp.zeros(carry_ref.shape, F32)

    reps = tk // LANES

    def weights(y, mask):
        sp = jnp.maximum(y, jnp.log2(1.0 + jnp.exp2(jnp.minimum(y, SOFTPLUS_CLAMP))))
        if mask is not None:
            sp = jnp.where(mask, sp, 0.0)
        hi = pltpu.bitcast(pltpu.bitcast(sp, jnp.uint32) & jnp.uint32(BF16_BITS), F32)
        lo = sp - hi
        split = jnp.concatenate([hi.astype(BF16), lo.astype(BF16)], axis=1)
        return _dot(split, tri_ref[...])

    def probs(y, total, mask):
        a = jnp.exp2(y - total)
        if mask is not None:
            a = jnp.where(mask, a, 0.0)
        return a.astype(BF16)

    def tile(use_mask):
        half = _half_masks()
        mask = None
        if stack_heads:
            rows = SB_HEADS * tq
            if use_mask:
                q_pos, k_pos = _positions(qi, kb, tq, tk, pos0)
                mask = jnp.concatenate([k_pos < q_pos] * SB_HEADS, axis=0)
            y = jnp.concatenate([_dot_nt(q_ref[h // 2] * half[h % 2], k_ref[h // 2])
                                 for h in range(SB_HEADS)], axis=0) * LOG2E
            suffix = weights(y, mask)
            carry = carry_ref[...].reshape(rows, LANES)
            a = probs(y, suffix + jnp.concatenate([carry] * reps, axis=1), mask)
            carry_ref[...] = (carry + jnp.broadcast_to(suffix[:, 0:1], (rows, LANES))).reshape(carry_ref.shape)
            for g in range(HEAD_GROUPS):
                v = v_ref[g]
                upd = (_dot(a[(2 * g) * tq:(2 * g + 1) * tq], v * half[0])
                       + _dot(a[(2 * g + 1) * tq:(2 * g + 2) * tq], v * half[1]))
                acc_ref[g] = acc_ref[g] + upd
            return
        if use_mask:
            q_pos, k_pos = _positions(qi, kb, tq, tk, pos0)
            mask = k_pos < q_pos

        def group(g, c):
            q = q_ref[g]
            k = k_ref[g]
            v = v_ref[g]
            upd = jnp.zeros((tq, LANES), F32)
            for j in range(2):
                y = _dot_nt(q * half[j], k) * LOG2E
                suffix = weights(y, mask)
                carry = carry_ref[2 * g + j]
                a = probs(y, suffix + jnp.concatenate([carry] * reps, axis=1), mask)
                upd = upd + _dot(a, v * half[j])
                carry_ref[2 * g + j] = carry + jnp.broadcast_to(suffix[:, 0:1], (tq, LANES))
            acc_ref[g] = acc_ref[g] + upd
            return c

        lax.fori_loop(0, HEAD_GROUPS, group, 0, unroll=True)

    @pl.when(masked_ref[s] == 1)
    def _():
        tile(True)

    @pl.when(masked_ref[s] == 0)
    def _():
        tile(False)

    @pl.when(last_ref[s] == 1)
    def _():
        for g in range(HEAD_GROUPS):
            o_ref[:, g * LANES:(g + 1) * LANES] = acc_ref[g].astype(BF16)


def _tri_ones_stacked(tk):
    r = np.arange(tk)
    tri = (r[:, None] >= r[None, :]).astype(np.float32)
    return jnp.asarray(np.concatenate([tri, tri], axis=0)).astype(BF16)


def _sb_attention(q16, k16, v16, *, tq, tk, pos0):
    B, _, S, _ = q16.shape
    n_q = S // tq
    sched = _schedule(n_q, tq, tk, pos0, lambda q_first, k_last: k_last < q_first)
    steps = int(sched[0].shape[0])
    qspec = pl.BlockSpec((None, HEAD_GROUPS, tq, LANES), lambda b, s, qi, kb, f, l, m: (b, 0, qi[s], 0))
    kspec = pl.BlockSpec((None, HEAD_GROUPS, tk, LANES), lambda b, s, qi, kb, f, l, m: (b, 0, kb[s], 0))
    grid_spec = pltpu.PrefetchScalarGridSpec(
        num_scalar_prefetch=5,
        grid=(B, steps),
        in_specs=[qspec, kspec, kspec,
                  pl.BlockSpec((2 * tk, tk), lambda b, s, qi, kb, f, l, m: (0, 0))],
        out_specs=pl.BlockSpec((tq, SB_W), lambda b, s, qi, kb, f, l, m: (b * n_q + qi[s], 0)),
        scratch_shapes=[pltpu.VMEM((HEAD_GROUPS, tq, LANES), F32),
                        pltpu.VMEM((SB_HEADS, tq, LANES), F32)],
    )
    return pl.pallas_call(
        functools.partial(_sb_kernel, tq=tq, tk=tk, pos0=pos0),
        grid_spec=grid_spec,
        out_shape=jax.ShapeDtypeStruct((B * S, SB_W), BF16),
        compiler_params=_params(("parallel", "arbitrary")),
        name="stick_breaking_attention",
    )(*sched, q16, k16, v16, _tri_ones_stacked(tk))


def _df_kernel(qi_ref, kb_ref, first_ref, last_ref, masked_ref, q_ref, k_ref, v_ref, lam_ref, g_ref,
               o_ref, m_ref, l_ref, acc_ref, *, tq, tk, pos0, lam_init):
    s = pl.program_id(1)
    qi = qi_ref[s]
    kb = kb_ref[s]

    @pl.when(first_ref[s] == 1)
    def _():
        m_ref[...] = jnp.full(m_ref.shape, NEG_BIG, F32)
        l_ref[...] = jnp.zeros(l_ref.shape, F32)
        acc_ref[...] = jnp.zeros(acc_ref.shape, F32)

    reps = tk // LANES

    def tile(use_mask):
        half = _half_masks()
        if use_mask:
            q_pos, k_pos = _positions(qi, kb, tq, tk, pos0)
            mask = k_pos <= (q_pos | (CHUNK - 1))

        def head(h, c):
            q = q_ref[h]
            k = k_ref[h]
            v = v_ref[h]
            for j in range(2):
                idx = 2 * h + j
                sc = _dot_nt(q * half[j], k)
                if use_mask:
                    sc = jnp.where(mask, sc, NEG_BIG)
                m_old = m_ref[idx]
                m_new = jnp.maximum(m_old, jnp.max(sc, axis=1, keepdims=True))
                alpha = jnp.exp(m_old - m_new)
                p = jnp.exp(sc - jnp.concatenate([m_new] * reps, axis=1))
                l_ref[idx] = alpha * l_ref[idx] + jnp.sum(p, axis=1, keepdims=True)
                acc_ref[idx] = alpha * acc_ref[idx] + _dot(p.astype(BF16), v)
                m_ref[idx] = m_new
            return c

        lax.fori_loop(0, DF_HEADS, head, 0, unroll=True)

    @pl.when(masked_ref[s] == 1)
    def _():
        tile(True)

    @pl.when(masked_ref[s] == 0)
    def _():
        tile(False)

    @pl.when(last_ref[s] == 1)
    def _():
        lv = lam_ref[...]
        lam = (jnp.exp(jnp.sum(lv[0:1] * lv[1:2], axis=1, keepdims=True))
               - jnp.exp(jnp.sum(lv[2:3] * lv[3:4], axis=1, keepdims=True)) + lam_init)
        gain = g_ref[...] * (1.0 - lam_init)
        for h in range(DF_HEADS):
            o = acc_ref[2 * h] / l_ref[2 * h] - lam * (acc_ref[2 * h + 1] / l_ref[2 * h + 1])
            o = o * lax.rsqrt(jnp.mean(o * o, axis=-1, keepdims=True) + RMS_EPS)
            o_ref[:, h * LANES:(h + 1) * LANES] = (o * gain).astype(BF16)


def _df_attention(q16, k16, v16, lam_vecs, subln_g, *, tq, tk, pos0, lam_init):
    B, _, S, _ = q16.shape
    n_q = S // tq
    sched = _schedule(n_q, tq, tk, pos0, lambda q_first, k_last: k_last <= (q_first | (CHUNK - 1)))
    steps = int(sched[0].shape[0])
    qspec = pl.BlockSpec((None, DF_HEADS, tq, LANES), lambda b, s, qi, kb, f, l, m: (b, 0, qi[s], 0))
    kspec = pl.BlockSpec((None, DF_HEADS, tk, LANES), lambda b, s, qi, kb, f, l, m: (b, 0, kb[s], 0))
    const = lambda b, s, qi, kb, f, l, m: (0, 0)
    grid_spec = pltpu.PrefetchScalarGridSpec(
        num_scalar_prefetch=5,
        grid=(B, steps),
        in_specs=[qspec, kspec, kspec,
                  pl.BlockSpec((4, DF_DIM), const), pl.BlockSpec((1, LANES), const)],
        out_specs=pl.BlockSpec((tq, DF_W), lambda b, s, qi, kb, f, l, m: (b * n_q + qi[s], 0)),
        scratch_shapes=[pltpu.VMEM((2 * DF_HEADS, tq, LANES), F32),
                        pltpu.VMEM((2 * DF_HEADS, tq, LANES), F32),
                        pltpu.VMEM((2 * DF_HEADS, tq, LANES), F32)],
    )
    return pl.pallas_call(
        functools.partial(_df_kernel, tq=tq, tk=tk, pos0=pos0, lam_init=lam_init),
        grid_spec=grid_spec,
        out_shape=jax.ShapeDtypeStruct((B * S, DF_W), BF16),
        compiler_params=_params(("parallel", "arbitrary")),
        name="differential_attention",
    )(*sched, q16, k16, v16, lam_vecs, subln_g.reshape(1, LANES))


def _merge_kernel(x_ref, oa_ref, ob_ref, sga_ref, sgb_ref, wa_ref, wb_ref, wo_ref, g_ref, b_ref, o_ref):
    ya = _dot(oa_ref[...], wa_ref[...])
    yb = _dot(ob_ref[...], wb_ref[...])
    mix = (sga_ref[...] * ya + sgb_ref[...] * yb).astype(BF16)
    h = DN_ALPHA * x_ref[...] + _dot(mix, wo_ref[...])
    o_ref[...] = _layer_norm(h, g_ref[...], b_ref[...])


def _merge(xf, oa, ob, sga, sgb, wa16, wb16, wo16, g, b):
    N, D = xf.shape
    tm = 256
    row = lambda i: (i, 0)
    const = lambda i: (0, 0)
    return pl.pallas_call(
        _merge_kernel,
        grid=(N // tm,),
        in_specs=[pl.BlockSpec((tm, D), row), pl.BlockSpec((tm, SB_W), row), pl.BlockSpec((tm, DF_W), row),
                  pl.BlockSpec((tm, D), row), pl.BlockSpec((tm, D), row),
                  pl.BlockSpec((SB_W, D), const), pl.BlockSpec((DF_W, D), const), pl.BlockSpec((D, D), const),
                  pl.BlockSpec((1, D), const), pl.BlockSpec((1, D), const)],
        out_specs=pl.BlockSpec((tm, D), row),
        out_shape=jax.ShapeDtypeStruct((N, D), F32),
        compiler_params=_params(("parallel",)),
        name="merge_ln1",
    )(xf, oa, ob, sga, sgb, wa16, wb16, wo16, g.reshape(1, D), b.reshape(1, D))


def _proj_kernel(x_ref, w_ref, o_ref):
    o_ref[...] = _dot(x_ref[...].astype(BF16), w_ref[...])


def _project(xf, w16):
    N, D = xf.shape
    tm = 256
    return pl.pallas_call(
        _proj_kernel,
        grid=(N // tm,),
        in_specs=[pl.BlockSpec((tm, D), lambda i: (i, 0)), pl.BlockSpec(w16.shape, lambda i: (0, 0))],
        out_specs=pl.BlockSpec((tm, w16.shape[1]), lambda i: (i, 0)),
        out_shape=jax.ShapeDtypeStruct((N, w16.shape[1]), F32),
        compiler_params=_params(("parallel",)),
        name="memory_projection",
    )(xf, w16)


def _memattn_kernel(x_ref, mk_ref, mv_ref, wq_ref, wo_ref, g_ref, b_ref, *rest):
    o_ref = rest[-1]
    x = x_ref[...]
    q = (_dot(x.astype(BF16), wq_ref[...]) * (MEM_DIM ** -0.5)).astype(BF16)
    outs = []
    for h in range(MEM_HEADS):
        sl = slice(h * MEM_DIM, (h + 1) * MEM_DIM)
        sc = _dot_nt(q[:, sl], mk_ref[:, sl].astype(BF16))
        sc = sc - jnp.max(sc, axis=-1, keepdims=True)
        p = jnp.exp(sc)
        p = p / jnp.sum(p, axis=-1, keepdims=True)
        outs.append(_dot(p.astype(BF16), mv_ref[:, sl].astype(BF16)).astype(BF16))
    o = jnp.concatenate(outs, axis=1)
    h2 = DN_ALPHA * x + _dot(o, wo_ref[...])
    o_ref[...] = _layer_norm(h2, g_ref[...], b_ref[...])


def _memory_attention(x1, mem_k, mem_v, wq16, wo16, g, b, *, batch, seq, out_rows, row_offset, prev=None):
    N, D = x1.shape
    M = mem_k.shape[1]
    tm = min(256, seq)
    spb = seq // tm
    off = row_offset // tm
    const = lambda bi, i: (0, 0)
    in_specs = [pl.BlockSpec((tm, D), lambda bi, i: (bi * spb + i, 0)),
                pl.BlockSpec((None, M, D), lambda bi, i: (bi, 0, 0)),
                pl.BlockSpec((None, M, D), lambda bi, i: (bi, 0, 0)),
                pl.BlockSpec((D, D), const), pl.BlockSpec((D, D), const),
                pl.BlockSpec((1, D), const), pl.BlockSpec((1, D), const)]
    args = [x1, mem_k, mem_v, wq16, wo16, g.reshape(1, D), b.reshape(1, D)]
    aliases = {}
    if prev is not None:
        in_specs.append(pl.BlockSpec(memory_space=pl.ANY))
        args.append(prev)
        aliases = {len(args) - 1: 0}
    return pl.pallas_call(
        _memattn_kernel,
        grid=(batch, spb),
        in_specs=in_specs,
        out_specs=pl.BlockSpec((tm, D), lambda bi, i: (off + bi * spb + i, 0)),
        out_shape=jax.ShapeDtypeStruct((out_rows, D), F32),
        input_output_aliases=aliases,
        compiler_params=_params(("parallel", "arbitrary")),
        name="memory_attention_ln2",
    )(*args)


def _router_kernel(x_ref, w_ref, b_ref, tri_ref, o_ref, cnt_ref, run_ref):
    @pl.when(pl.program_id(0) == 0)
    def _():
        run_ref[...] = jnp.zeros(run_ref.shape, F32)

    x = x_ref[...]
    tm = x.shape[0]
    x_hi = x.astype(BF16)
    x_lo = (x - x_hi.astype(F32)).astype(BF16)
    w = w_ref[...]
    w_hi = w.astype(BF16)
    w_lo = (w - w_hi.astype(F32)).astype(BF16)
    logits = _dot(x_hi, w_hi) + _dot(x_lo, w_hi) + _dot(x_hi, w_lo) + b_ref[...]
    lane = lax.broadcasted_iota(jnp.int32, (tm, LANES), 1)
    big = jnp.int32(LANES)
    glog = jnp.where(lane < N_GROUPS, logits, -jnp.inf)
    gmax = jnp.max(glog, axis=-1, keepdims=True)
    grp = jnp.min(jnp.where(glog == gmax, lane, big), axis=-1, keepdims=True)
    g_prob = 1.0 / jnp.sum(jnp.exp(glog - gmax), axis=-1, keepdims=True)
    lo = N_GROUPS + grp * EXPERTS_PER_GROUP
    in_grp = (lane >= lo) & (lane < lo + EXPERTS_PER_GROUP)
    el = jnp.where(in_grp, logits, -jnp.inf)
    v1 = jnp.max(el, axis=-1, keepdims=True)
    i1 = jnp.min(jnp.where(el == v1, lane, big), axis=-1, keepdims=True)
    el2 = jnp.where(lane == i1, -jnp.inf, el)
    v2 = jnp.max(el2, axis=-1, keepdims=True)
    i2 = jnp.min(jnp.where(el2 == v2, lane, big), axis=-1, keepdims=True)
    e2 = jnp.exp(v2 - v1)
    den = 1.0 / (1.0 + e2)
    g1 = den * g_prob
    g2 = e2 * den * g_prob
    hit1 = lane == i1
    hit2 = lane == i2
    member = jnp.where(hit1 | hit2, 1.0, 0.0)
    before = _dot(tri_ref[...], member.astype(BF16)) + run_ref[...]
    r1 = jnp.sum(jnp.where(hit1, before, 0.0), axis=-1, keepdims=True)
    r2 = jnp.sum(jnp.where(hit2, before, 0.0), axis=-1, keepdims=True)
    run = run_ref[...] + jnp.sum(member, axis=0, keepdims=True)
    run_ref[...] = run
    cnt_ref[...] = run
    out = jnp.where(lane == 0, (i1 - N_GROUPS).astype(F32), 0.0)
    out = jnp.where(lane == 1, (i2 - N_GROUPS).astype(F32), out)
    out = jnp.where(lane == 2, g1, out)
    out = jnp.where(lane == 3, g2, out)
    out = jnp.where(lane == 4, r1, out)
    out = jnp.where(lane == 5, r2, out)
    o_ref[...] = out


def _router(x2, w_group, b_group, w_router, b_router):
    N, D = x2.shape
    tm = 256
    w = jnp.zeros((D, LANES), F32).at[:, :N_GROUPS].set(w_group).at[:, N_GROUPS:N_GROUPS + N_EXPERTS].set(w_router)
    b = jnp.zeros((1, LANES), F32).at[0, :N_GROUPS].set(b_group).at[0, N_GROUPS:N_GROUPS + N_EXPERTS].set(b_router)
    r_idx = np.arange(tm)
    strict_lower = jnp.asarray((r_idx[:, None] > r_idx[None, :]).astype(np.float32)).astype(BF16)
    r, cnt = pl.pallas_call(
        _router_kernel,
        grid=(N // tm,),
        in_specs=[pl.BlockSpec((tm, D), lambda i: (i, 0)), pl.BlockSpec((D, LANES), lambda i: (0, 0)),
                  pl.BlockSpec((1, LANES), lambda i: (0, 0)), pl.BlockSpec((tm, tm), lambda i: (0, 0))],
        out_specs=[pl.BlockSpec((tm, LANES), lambda i: (i, 0)), pl.BlockSpec((1, LANES), lambda i: (0, 0))],
        out_shape=[jax.ShapeDtypeStruct((N, LANES), F32), jax.ShapeDtypeStruct((1, LANES), F32)],
        scratch_shapes=[pltpu.VMEM((1, LANES), F32)],
        compiler_params=_params(("arbitrary",)),
        name="router",
    )(x2, w, b, strict_lower)
    expert = r[:, 0:2].astype(jnp.int32)
    gate = r[:, 2:4]
    rank = r[:, 4:6].astype(jnp.int32)
    counts = cnt[0, N_GROUPS:N_GROUPS + N_EXPERTS].astype(jnp.int32)
    return expert, gate, rank, counts


def _row_scatter(idx_ref, k, src_ref, dst_hbm, sem, rows):
    for r in range(rows):
        p = idx_ref[0, k, r]
        pltpu.make_async_copy(src_ref.at[pl.ds(r, 1)], dst_hbm.at[pl.ds(p, 1)], sem).start(priority=r % 2)


def _dispatch_kernel(pos_ref, x_ref, xs_in_hbm, xs_hbm, buf_ref, sem_ref, *, tm, n_tiles):
    del xs_in_hbm
    i = pl.program_id(0)
    slot = i % 2

    def wait(sl):
        for k in range(TOP_K):
            pltpu.make_async_copy(buf_ref.at[sl], xs_hbm.at[pl.ds(0, tm)], sem_ref.at[sl]).wait()

    @pl.when(i >= 2)
    def _():
        wait(slot)

    buf_ref[slot] = x_ref[...]
    for k in range(TOP_K):
        _row_scatter(pos_ref, k, buf_ref.at[slot], xs_hbm, sem_ref.at[slot], tm)

    @pl.when(i == n_tiles - 1)
    def _():
        wait(slot)
        if n_tiles > 1:
            wait(1 - slot)


def _dispatch(x2, pos3, rows):
    N, D = x2.shape
    nt, _, tm = pos3.shape
    xs0 = jnp.zeros((rows, D), F32)
    return pl.pallas_call(
        functools.partial(_dispatch_kernel, tm=tm, n_tiles=nt),
        grid=(nt,),
        in_specs=[pl.BlockSpec((1, TOP_K, tm), lambda i: (i, 0, 0), memory_space=pltpu.SMEM),
                  pl.BlockSpec((tm, D), lambda i: (i, 0)),
                  pl.BlockSpec(memory_space=pl.ANY)],
        out_specs=pl.BlockSpec(memory_space=pl.ANY),
        out_shape=jax.ShapeDtypeStruct((rows, D), F32),
        scratch_shapes=[pltpu.VMEM((2, tm, D), F32), pltpu.SemaphoreType.DMA((2,))],
        input_output_aliases={2: 0},
        compiler_params=_params(("arbitrary",)),
        name="expert_dispatch",
    )(pos3, x2, xs0)


def _row_gather(idx_ref, k, src_hbm, dst_ref, sem, rows):
    for r in range(rows):
        t = idx_ref[0, k, r]
        pltpu.make_async_copy(src_hbm.at[pl.ds(t, 1)], dst_ref.at[pl.ds(r, 1)], sem).start(priority=r % 2)


def _gather_wait(src_hbm, dst_ref, sem, rows):
    pltpu.make_async_copy(src_hbm.at[pl.ds(0, rows)], dst_ref, sem).wait()


def _expert_kernel(be_ref, used_ref, x_ref, wg_ref, wu_ref, wd_ref, o_ref):
    i = pl.program_id(0)

    @pl.when(i < used_ref[0])
    def _():
        xb = x_ref[...].astype(BF16)
        hg = _dot(xb, wg_ref[...])
        hu = _dot(xb, wu_ref[...])
        hid = (hg * jax.nn.sigmoid(hg) * hu).astype(BF16)
        o_ref[...] = _dot(hid, wd_ref[...])

    @pl.when(i >= used_ref[0])
    def _():
        o_ref[...] = jnp.zeros(o_ref.shape, F32)


def _experts(xs, blk_e, n_used, wg16, wu16, wd16):
    R, D = xs.shape
    nb = R // EBLOCK
    grid_spec = pltpu.PrefetchScalarGridSpec(
        num_scalar_prefetch=2,
        grid=(nb,),
        in_specs=[pl.BlockSpec((EBLOCK, D), lambda i, be, nu: (i, 0)),
                  pl.BlockSpec((None, D, D_EXPERT), lambda i, be, nu: (be[i], 0, 0)),
                  pl.BlockSpec((None, D, D_EXPERT), lambda i, be, nu: (be[i], 0, 0)),
                  pl.BlockSpec((None, D_EXPERT, D), lambda i, be, nu: (be[i], 0, 0))],
        out_specs=pl.BlockSpec((EBLOCK, D), lambda i, be, nu: (i, 0)),
    )
    return pl.pallas_call(
        _expert_kernel,
        grid_spec=grid_spec,
        out_shape=jax.ShapeDtypeStruct((R, D), F32),
        compiler_params=_params(("arbitrary",)),
        name="grouped_experts",
    )(blk_e, n_used, xs, wg16, wu16, wd16)


def _combine_kernel(pos_cur_ref, pos_next_ref, x_ref, gate_ref, g_ref, b_ref, y_hbm, oa_ref, ob_ref,
                    buf_ref, sem_ref, *, tm, n_tiles, n_tiles_a):
    i = pl.program_id(0)
    slot = i % 2

    def gather(pos_ref, sl):
        for k in range(TOP_K):
            _row_gather(pos_ref, k, y_hbm, buf_ref.at[sl, k], sem_ref.at[sl], tm)

    def wait(sl):
        for k in range(TOP_K):
            _gather_wait(y_hbm, buf_ref.at[sl, k], sem_ref.at[sl], tm)

    @pl.when(i == 0)
    def _():
        gather(pos_cur_ref, 0)

    wait(slot)
    gather(pos_next_ref, 1 - slot)
    gate = gate_ref[...]
    moe = buf_ref[slot, 0] * gate[:, 0:1] + buf_ref[slot, 1] * gate[:, 1:2]
    h = DN_ALPHA * x_ref[...] + moe
    res = _layer_norm(h, g_ref[...], b_ref[...])

    @pl.when(i < n_tiles_a)
    def _():
        oa_ref[...] = res

    @pl.when(i >= n_tiles_a)
    def _():
        ob_ref[...] = res

    @pl.when(i == n_tiles - 1)
    def _():
        wait(1 - slot)


def _combine(x2, ys, pos3, gate, g, b, rows_a):
    N, D = x2.shape
    nt, _, tm = pos3.shape
    nta = rows_a // tm
    gate_p = jnp.zeros((N, LANES), F32).at[:, 0:TOP_K].set(gate)
    return pl.pallas_call(
        functools.partial(_combine_kernel, tm=tm, n_tiles=nt, n_tiles_a=nta),
        grid=(nt,),
        in_specs=[pl.BlockSpec((1, TOP_K, tm), lambda i: (i, 0, 0), memory_space=pltpu.SMEM),
                  pl.BlockSpec((1, TOP_K, tm), lambda i: (jnp.minimum(i + 1, nt - 1), 0, 0),
                               memory_space=pltpu.SMEM),
                  pl.BlockSpec((tm, D), lambda i: (i, 0)),
                  pl.BlockSpec((tm, LANES), lambda i: (i, 0)),
                  pl.BlockSpec((1, D), lambda i: (0, 0)), pl.BlockSpec((1, D), lambda i: (0, 0)),
                  pl.BlockSpec(memory_space=pl.ANY)],
        out_specs=[pl.BlockSpec((tm, D), lambda i: (jnp.minimum(i, nta - 1), 0)),
                   pl.BlockSpec((tm, D), lambda i: (jnp.maximum(i - nta, 0), 0))],
        out_shape=[jax.ShapeDtypeStruct((rows_a, D), F32), jax.ShapeDtypeStruct((N - rows_a, D), F32)],
        scratch_shapes=[pltpu.VMEM((2, TOP_K, tm, D), F32), pltpu.SemaphoreType.DMA((2,))],
        compiler_params=_params(("arbitrary",)),
        name="combine_ln3",
    )(pos3, pos3, x2, gate_p, g.reshape(1, D), b.reshape(1, D), ys)


def _dispatch_plan(expert, rank, counts):
    N, K = expert.shape
    A = N * K
    E = N_EXPERTS
    padded = (counts + EBLOCK - 1) // EBLOCK * EBLOCK
    pad_end = jnp.cumsum(padded)
    pad_start = pad_end - padded
    n_blocks = (A + E * (EBLOCK - 1) + EBLOCK - 1) // EBLOCK
    ids = jnp.arange(E, dtype=jnp.int32)
    pos = (jnp.sum(jnp.where(expert[..., None] == ids, pad_start, 0), axis=-1) + rank).astype(jnp.int32)
    blk_start = jnp.arange(n_blocks, dtype=jnp.int32) * EBLOCK
    blk_e = jnp.minimum(jnp.sum((pad_end[None, :] <= blk_start[:, None]).astype(jnp.int32), axis=1), E - 1)
    n_used = (pad_end[-1] // EBLOCK).astype(jnp.int32).reshape(1)
    return pos, blk_e.astype(jnp.int32), n_used, n_blocks * EBLOCK


def _hier_moe_ln3(x2, rows_a, w_group, b_group, w_router, b_router, wg16, wu16, wd16, g, b):
    N = x2.shape[0]
    tm = 256
    expert, gate, rank, counts = _router(x2, w_group, b_group, w_router, b_router)
    pos, blk_e, n_used, rows = _dispatch_plan(expert, rank, counts)
    pos3 = pos.reshape(N // tm, tm, TOP_K).transpose(0, 2, 1)
    xs = _dispatch(x2, pos3, rows)
    ys = _experts(xs, blk_e, n_used, wg16, wu16, wd16)
    return _combine(x2, ys, pos3, gate, g, b, rows_a)


def _mixers(x, caches, lam_init, w):
    B, S, D = x.shape
    N = B * S
    xf = x.reshape(N, D)
    past = 0 if caches is None else caches[0].shape[1]
    (ka, va, kb, vb, sga, sgb, qa16, ka16, va16, qb16, kb16, vb16) = _in_projection(x, w['w_in'], past)
    if caches is None:
        tq = min(512, S)
        tk = min(256, S)
        keys = (ka16, va16, kb16, vb16)
    else:
        tq = S
        tk = 256
        keys = tuple(_pack_keys(c.reshape(B, past, -1), n, past + tk)
                     for c, n in zip(caches, (ka16, va16, kb16, vb16)))
    oa = _sb_attention(qa16, keys[0], keys[1], tq=tq, tk=tk, pos0=past)
    tk_df = past + tk if caches is not None else min(1024, S)
    ob = _df_attention(qb16, keys[2], keys[3], w['lam'], w['subln_g'], tq=tq, tk=tk_df, pos0=past,
                       lam_init=lam_init)
    x1 = _merge(xf, oa, ob, sga, sgb, w['w_branch_a'], w['w_branch_b'], w['w_out'], w['ln1_g'], w['ln1_b'])
    new_rows = (ka.reshape(B, S, SB_HEADS, SB_DIM), va.reshape(B, S, SB_HEADS, SB_DIM),
                kb.reshape(B, S, DF_HEADS, 2 * DF_DIM), vb.reshape(B, S, DF_HEADS, 2 * DF_DIM))
    return x1, new_rows


def kernel(x_prompt, mem_prompt, x_sample, cache_sb_k, cache_sb_v, cache_diff_k, cache_diff_v, cache_mem_k, cache_mem_v, w_in, lam_q1, lam_k1, lam_q2, lam_k2, subln_g, w_branch_a, w_branch_b, w_out, ln1_g, ln1_b, w_mq, w_mk, w_mv, w_mo, ln2_g, ln2_b, w_group, b_group, w_router, b_router, w_up, w_gate, w_down, ln3_g, ln3_b):
    depth = w_in.shape[0]
    yp, ys = x_prompt, x_sample
    Bp, n_mem = mem_prompt.shape[0], mem_prompt.shape[1]
    Bs = x_sample.shape[0]
    outs = [[] for _ in range(10)]
    for l in range(depth):
        lam_init = 0.8 - 0.6 * math.exp(-0.3 * l)
        w = {
            'w_in': w_in[l].astype(BF16),
            'lam': jnp.stack([lam_q1[l], lam_k1[l], lam_q2[l], lam_k2[l]]).astype(F32),
            'subln_g': subln_g[l],
            'w_branch_a': w_branch_a[l].astype(BF16), 'w_branch_b': w_branch_b[l].astype(BF16),
            'w_out': w_out[l].astype(BF16), 'ln1_g': ln1_g[l], 'ln1_b': ln1_b[l],
            'w_mq': w_mq[l].astype(BF16), 'w_mo': w_mo[l].astype(BF16),
            'ln2_g': ln2_g[l], 'ln2_b': ln2_b[l],
            'w_group': w_group[l], 'b_group': b_group[l], 'w_router': w_router[l], 'b_router': b_router[l],
            'w_gate': w_gate[l].astype(BF16), 'w_up': w_up[l].astype(BF16), 'w_down': w_down[l].astype(BF16),
            'ln3_g': ln3_g[l], 'ln3_b': ln3_b[l],
        }
        memf = mem_prompt.reshape(Bp * n_mem, D_MODEL)
        mk = _project(memf, w_mk[l].astype(BF16)).reshape(Bp, n_mem, D_MODEL)
        mv = _project(memf, w_mv[l].astype(BF16)).reshape(Bp, n_mem, D_MODEL)
        Np = Bp * yp.shape[1]
        Ns = Bs * ys.shape[1]
        x1p, rows_p = _mixers(yp, None, lam_init, w)
        for lst, val in zip(outs[0:6], rows_p + (mk.reshape(Bp, n_mem, MEM_HEADS, MEM_DIM),
                                               mv.reshape(Bp, n_mem, MEM_HEADS, MEM_DIM))):
            lst.append(val)
        caches = (cache_sb_k[l], cache_sb_v[l], cache_diff_k[l], cache_diff_v[l])
        x1s, rows_s = _mixers(ys, caches, lam_init, w)
        for lst, val in zip(outs[6:10], rows_s):
            lst.append(val)
        x2 = _memory_attention(x1p, mk, mv, w['w_mq'], w['w_mo'], w['ln2_g'], w['ln2_b'],
                               batch=Bp, seq=yp.shape[1], out_rows=Np + Ns, row_offset=0,
                               prev=jnp.zeros((Np + Ns, D_MODEL), F32))
        x2 = _memory_attention(x1s, cache_mem_k[l].reshape(Bs, n_mem, D_MODEL),
                               cache_mem_v[l].reshape(Bs, n_mem, D_MODEL),
                               w['w_mq'], w['w_mo'], w['ln2_g'], w['ln2_b'],
                               batch=Bs, seq=ys.shape[1], out_rows=Np + Ns, row_offset=Np, prev=x2)
        x3p, x3s = _hier_moe_ln3(x2, Np, w['w_group'], w['b_group'], w['w_router'], w['b_router'],
                                 w['w_gate'], w['w_up'], w['w_down'], w['ln3_g'], w['ln3_b'])
        yp = x3p.reshape(yp.shape)
        ys = x3s.reshape(ys.shape)
    return (yp, ys) + tuple(jnp.stack(o) for o in outs)
```

```python
import functools
import math

import jax
import jax.numpy as jnp
import numpy as np
from jax import lax
from jax.experimental import pallas as pl
from jax.experimental.pallas import tpu as pltpu

F32 = jnp.float32
BF16 = jnp.bfloat16

D_MODEL = 1024
SB_HEADS = 8
SB_DIM = 64
SB_W = SB_HEADS * SB_DIM
DF_HEADS = 4
DF_DIM = 64
DF_W = DF_HEADS * 2 * DF_DIM
CHUNK = 64
MEM_HEADS = 4
MEM_DIM = D_MODEL // MEM_HEADS
N_GROUPS = 4
EXPERTS_PER_GROUP = 8
N_EXPERTS = N_GROUPS * EXPERTS_PER_GROUP
TOP_K = 2
D_EXPERT = 512
EBLOCK = 256
ROPE_THETA = 10000.0
LN_EPS = 1e-5
RMS_EPS = 1e-5
DEPTH = 1
DN_ALPHA = (2 * DEPTH) ** 0.25
IN_W = 3 * SB_W + 3 * DF_W + 2 * D_MODEL

LANES = 128
HEAD_GROUPS = 4
LOG2E = 1.4426950408889634
NEG_BIG = -1e30
VMEM_LIMIT = 48 * 1024 * 1024


def _dot(a, b):
    return jnp.dot(a, b, preferred_element_type=F32)


def _dot_nt(a, b):
    return lax.dot_general(a, b, (((1,), (1,)), ((), ())), preferred_element_type=F32)


def _layer_norm(h, g, b):
    mu = jnp.mean(h, axis=-1, keepdims=True)
    d = h - mu
    var = jnp.mean(d * d, axis=-1, keepdims=True)
    return d * lax.rsqrt(var + LN_EPS) * g + b


def _params(sem, vmem=VMEM_LIMIT):
    return pltpu.CompilerParams(dimension_semantics=sem, vmem_limit_bytes=vmem)


def _inproj_kernel(x_ref, w_ref, cos_ref, sina_ref, sinb_ref,
                   ka_ref, va_ref, kb_ref, vb_ref, sga_ref, sgb_ref,
                   qa16_ref, ka16_ref, va16_ref, qb16_ref, kb16_ref, vb16_ref):
    xb = x_ref[...].astype(BF16)

    def mm(off, width):
        return _dot(xb, w_ref[:, off:off + width])

    def put_groups(ref, val):
        for g in range(HEAD_GROUPS):
            ref[g] = val[:, g * LANES:(g + 1) * LANES].astype(BF16)

    lane = lax.broadcasted_iota(jnp.int32, (1, LANES), 1)
    keep = (jnp.where(lane < LANES // 2, 1.0, 0.0), jnp.where(lane >= LANES // 2, 1.0, 0.0))

    def put_halves(ref, val):
        for g in range(HEAD_GROUPS):
            slab = val[:, g * LANES:(g + 1) * LANES]
            for j in range(2):
                ref[2 * g + j] = (slab * keep[j]).astype(BF16)

    def put_head_rows(ref, val):
        for h in range(DF_HEADS):
            ref[pl.ds(h, val.shape[0], stride=DF_HEADS), :] = val[:, h * LANES:(h + 1) * LANES]

    def rope(t):
        n = t.shape[1]
        reps = n // LANES
        cos = jnp.concatenate([cos_ref[...]] * reps, axis=1)
        sina = jnp.concatenate([sina_ref[...]] * reps, axis=1)
        sinb = jnp.concatenate([sinb_ref[...]] * reps, axis=1)
        fwd = pltpu.roll(t, n - DF_DIM // 2, 1)
        bwd = pltpu.roll(t, DF_DIM // 2, 1)
        return t * cos + fwd * sina + bwd * sinb

    qa = mm(0, SB_W)
    put_halves(qa16_ref, qa * (LOG2E * SB_DIM ** -0.5))
    ka = mm(SB_W, SB_W)
    ka_ref[...] = ka
    put_groups(ka16_ref, ka)
    va = mm(2 * SB_W, SB_W)
    va_ref[...] = va
    put_halves(va16_ref, va)
    off = 3 * SB_W
    qb = rope(mm(off, DF_W))
    put_halves(qb16_ref, qb * (LOG2E * DF_DIM ** -0.5))
    kb = rope(mm(off + DF_W, DF_W))
    put_head_rows(kb_ref, kb)
    put_groups(kb16_ref, kb)
    vb = mm(off + 2 * DF_W, DF_W)
    put_head_rows(vb_ref, vb)
    put_groups(vb16_ref, vb)
    off = off + 3 * DF_W
    sga_ref[...] = jax.nn.sigmoid(mm(off, D_MODEL))
    sgb_ref[...] = jax.nn.sigmoid(mm(off + D_MODEL, D_MODEL))


def _rope_tables(pos0, seq):
    half = DF_DIM // 2
    inv_freq = jnp.exp(jnp.arange(half, dtype=F32) * (-math.log(ROPE_THETA) / half))
    ang = (pos0 + jnp.arange(seq, dtype=jnp.int32)).astype(F32)[:, None] * inv_freq[None, :]
    cos, sin = jnp.cos(ang), jnp.sin(ang)
    zero = jnp.zeros_like(sin)
    reps = LANES // DF_DIM
    cos_t = jnp.tile(jnp.concatenate([cos, cos], axis=1), (1, reps))
    sina_t = jnp.tile(jnp.concatenate([-sin, zero], axis=1), (1, reps))
    sinb_t = jnp.tile(jnp.concatenate([zero, sin], axis=1), (1, reps))
    return cos_t, sina_t, sinb_t


def _in_projection(x, w_in16, pos0):
    B, S, D = x.shape
    N = B * S
    tm = min(256, S)
    spb = S // tm
    xf = x.reshape(N, D)
    cos_t, sina_t, sinb_t = _rope_tables(pos0, S)
    row = lambda i: (i, 0)
    tab = lambda i: (i % spb, 0)
    grp = lambda i: (i // spb, 0, i % spb, 0)
    f32_512 = jax.ShapeDtypeStruct((N, SB_W), F32)
    f32_1024 = jax.ShapeDtypeStruct((N, D), F32)
    g16 = jax.ShapeDtypeStruct((B, HEAD_GROUPS, S, LANES), BF16)
    h16 = jax.ShapeDtypeStruct((B, 2 * HEAD_GROUPS, S, LANES), BF16)
    f32_rows = jax.ShapeDtypeStruct((N * DF_HEADS, LANES), F32)
    rspec = pl.BlockSpec((tm * DF_HEADS, LANES), row)
    spec512 = pl.BlockSpec((tm, SB_W), row)
    spec1024 = pl.BlockSpec((tm, D), row)
    gspec = pl.BlockSpec((None, HEAD_GROUPS, tm, LANES), grp)
    hspec = pl.BlockSpec((None, 2 * HEAD_GROUPS, tm, LANES), grp)
    tspec = pl.BlockSpec((tm, LANES), tab)
    return pl.pallas_call(
        _inproj_kernel,
        grid=(N // tm,),
        in_specs=[spec1024, pl.BlockSpec((D, IN_W), lambda i: (0, 0)), tspec, tspec, tspec],
        out_specs=[spec512, spec512, rspec, rspec, spec1024, spec1024,
                   hspec, gspec, hspec, hspec, gspec, gspec],
        out_shape=[f32_512, f32_512, f32_rows, f32_rows, f32_1024, f32_1024,
                   h16, g16, h16, h16, g16, g16],
        compiler_params=_params(("parallel",), 56 * 1024 * 1024),
        name="in_projection",
    )(xf, w_in16, cos_t, sina_t, sinb_t)


def _pack_kernel(c_ref, n_ref, o_ref, *, past, new_rows, halves, head_rows):
    n_slabs = o_ref.shape[0]
    if head_rows:
        for h in range(n_slabs):
            o_ref[h, 0:past, :] = c_ref[pl.ds(h, past, stride=n_slabs), :].astype(BF16)
    else:
        c = c_ref[...]
        lane = lax.broadcasted_iota(jnp.int32, (1, LANES), 1)
        keep = (jnp.where(lane < LANES // 2, 1.0, 0.0), jnp.where(lane >= LANES // 2, 1.0, 0.0))
        for g in range(HEAD_GROUPS):
            slab = c[:, g * LANES:(g + 1) * LANES]
            if halves:
                for j in range(2):
                    o_ref[2 * g + j, 0:past, :] = (slab * keep[j]).astype(BF16)
            else:
                o_ref[g, 0:past, :] = slab.astype(BF16)
    pad = o_ref.shape[1] - past - new_rows
    o_ref[:, past:past + new_rows, :] = n_ref[...]
    o_ref[:, past + new_rows:, :] = jnp.zeros((n_slabs, pad, LANES), BF16)


def _pack_keys(cache, new16, total):
    B, R, W = cache.shape
    n, S = new16.shape[1], new16.shape[2]
    head_rows = W == LANES
    P = R // n if head_rows else R
    return pl.pallas_call(
        functools.partial(_pack_kernel, past=P, new_rows=S, halves=(n == 2 * HEAD_GROUPS), head_rows=head_rows),
        grid=(B,),
        in_specs=[pl.BlockSpec((None, R, W), lambda b: (b, 0, 0)),
                  pl.BlockSpec((None, n, S, LANES), lambda b: (b, 0, 0, 0))],
        out_specs=pl.BlockSpec((None, n, total, LANES), lambda b: (b, 0, 0, 0)),
        out_shape=jax.ShapeDtypeStruct((B, n, total, LANES), BF16),
        compiler_params=_params(("parallel",)),
        name="pack_keys",
    )(cache, new16)


def _schedule(n_q, tq, tk, pos0, fully_visible):
    qi, kb, first, last, masked = [], [], [], [], []
    for i in range(n_q):
        top = -(-(pos0 + (i + 1) * tq) // tk) - 1
        for j in range(top, -1, -1):
            qi.append(i)
            kb.append(j)
            first.append(1 if j == top else 0)
            last.append(1 if j == 0 else 0)
            masked.append(0 if fully_visible(pos0 + i * tq, (j + 1) * tk - 1) else 1)
    mk = lambda v: jnp.asarray(np.asarray(v, np.int32))
    return mk(qi), mk(kb), mk(first), mk(last), mk(masked)


def _positions(qi, kb, tq, tk, pos0):
    q_pos = pos0 + qi * tq + lax.broadcasted_iota(jnp.int32, (tq, 1), 0)
    k_pos = kb * tk + lax.broadcasted_iota(jnp.int32, (1, tk), 1)
    return q_pos, k_pos


SOFTPLUS_CLAMP = 64.0
STACKED_ROWS_MAX = 512
BF16_BITS = 0xFFFF0000


def _sb_kernel(qi_ref, kb_ref, first_ref, last_ref, masked_ref, q_ref, k_ref, v_ref, tri_ref, o_ref,
               acc_ref, carry_ref, *, tq, tk, pos0):
    stack_heads = SB_HEADS * tq <= STACKED_ROWS_MAX
    s = pl.program_id(1)
    qi = qi_ref[s]
    kb = kb_ref[s]

    @pl.when(first_ref[s] == 1)
    def _():
        acc_ref[...] = jnp.zeros(acc_ref.shape, F32)
        carry_ref[...] = jnp.zeros(carry_ref.shape, F32)

    reps = tk // LANES

    def weights(y, mask):
        sp = jnp.maximum(y, jnp.log2(1.0 + jnp.exp2(jnp.minimum(y, SOFTPLUS_CLAMP))))
        if mask is not None:
            sp = jnp.where(mask, sp, 0.0)
        hi = pltpu.bitcast(pltpu.bitcast(sp, jnp.uint32) & jnp.uint32(BF16_BITS), F32)
        lo = sp - hi
        split = jnp.concatenate([hi.astype(BF16), lo.astype(BF16)], axis=1)
        return _dot(split, tri_ref[...])

    def probs(y, total, mask):
        a = jnp.exp2(y - total)
        if mask is not None:
            a = jnp.where(mask, a, 0.0)
        return a.astype(BF16)

    def tile(use_mask):
        mask = None
        if stack_heads:
            rows = SB_HEADS * tq
            if use_mask:
                q_pos, k_pos = _positions(qi, kb, tq, tk, pos0)
                mask = jnp.concatenate([k_pos < q_pos] * SB_HEADS, axis=0)
            y = jnp.concatenate([_dot_nt(q_ref[h], k_ref[h // 2]) for h in range(SB_HEADS)], axis=0)
            suffix = weights(y, mask)
            carry = carry_ref[...].reshape(rows, LANES)
            a = probs(y, suffix + jnp.concatenate([carry] * reps, axis=1), mask)
            carry_ref[...] = (carry + jnp.broadcast_to(suffix[:, 0:1], (rows, LANES))).reshape(carry_ref.shape)
            for g in range(HEAD_GROUPS):
                upd = (_dot(a[(2 * g) * tq:(2 * g + 1) * tq], v_ref[2 * g])
                       + _dot(a[(2 * g + 1) * tq:(2 * g + 2) * tq], v_ref[2 * g + 1]))
                acc_ref[g] = acc_ref[g] + upd
            return
        if use_mask:
            q_pos, k_pos = _positions(qi, kb, tq, tk, pos0)
            mask = k_pos < q_pos

        def group(g, c):
            k = k_ref[g]
            upd = jnp.zeros((tq, LANES), F32)
            for j in range(2):
                y = _dot_nt(q_ref[2 * g + j], k)
                suffix = weights(y, mask)
                carry = carry_ref[2 * g + j]
                a = probs(y, suffix + jnp.concatenate([carry] * reps, axis=1), mask)
                upd = upd + _dot(a, v_ref[2 * g + j])
                carry_ref[2 * g + j] = carry + jnp.broadcast_to(suffix[:, 0:1], (tq, LANES))
            acc_ref[g] = acc_ref[g] + upd
            return c

        lax.fori_loop(0, HEAD_GROUPS, group, 0, unroll=True)

    @pl.when(masked_ref[s] == 1)
    def _():
        tile(True)

    @pl.when(masked_ref[s] == 0)
    def _():
        tile(False)

    @pl.when(last_ref[s] == 1)
    def _():
        for g in range(HEAD_GROUPS):
            o_ref[:, g * LANES:(g + 1) * LANES] = acc_ref[g].astype(BF16)


def _tri_ones_stacked(tk):
    r = np.arange(tk)
    tri = (r[:, None] >= r[None, :]).astype(np.float32)
    return jnp.asarray(np.concatenate([tri, tri], axis=0)).astype(BF16)


def _sb_attention(q16, k16, v16, *, tq, tk, pos0):
    B, _, S, _ = q16.shape
    n_q = S // tq
    sched = _schedule(n_q, tq, tk, pos0, lambda q_first, k_last: k_last < q_first)
    steps = int(sched[0].shape[0])
    qspec = pl.BlockSpec((None, SB_HEADS, tq, LANES), lambda b, s, qi, kb, f, l, m: (b, 0, qi[s], 0))
    kspec = pl.BlockSpec((None, HEAD_GROUPS, tk, LANES), lambda b, s, qi, kb, f, l, m: (b, 0, kb[s], 0))
    vspec = pl.BlockSpec((None, SB_HEADS, tk, LANES), lambda b, s, qi, kb, f, l, m: (b, 0, kb[s], 0))
    grid_spec = pltpu.PrefetchScalarGridSpec(
        num_scalar_prefetch=5,
        grid=(B, steps),
        in_specs=[qspec, kspec, vspec,
                  pl.BlockSpec((2 * tk, tk), lambda b, s, qi, kb, f, l, m: (0, 0))],
        out_specs=pl.BlockSpec((tq, SB_W), lambda b, s, qi, kb, f, l, m: (b * n_q + qi[s], 0)),
        scratch_shapes=[pltpu.VMEM((HEAD_GROUPS, tq, LANES), F32),
                        pltpu.VMEM((SB_HEADS, tq, LANES), F32)],
    )
    return pl.pallas_call(
        functools.partial(_sb_kernel, tq=tq, tk=tk, pos0=pos0),
        grid_spec=grid_spec,
        out_shape=jax.ShapeDtypeStruct((B * S, SB_W), BF16),
        compiler_params=_params(("parallel", "arbitrary")),
        name="stick_breaking_attention",
    )(*sched, q16, k16, v16, _tri_ones_stacked(tk))


def _df_kernel(qi_ref, kb_ref, first_ref, last_ref, masked_ref, q_ref, k_ref, v_ref, lam_ref, g_ref,
               o_ref, m_ref, l_ref, acc_ref, *, tq, tk, pos0, lam_init):
    s = pl.program_id(1)
    qi = qi_ref[s]
    kb = kb_ref[s]

    @pl.when(first_ref[s] == 1)
    def _():
        m_ref[...] = jnp.full(m_ref.shape, NEG_BIG, F32)
        l_ref[...] = jnp.zeros(l_ref.shape, F32)
        acc_ref[...] = jnp.zeros(acc_ref.shape, F32)

    reps = tk // LANES

    def tile(use_mask):
        if use_mask:
            q_pos, k_pos = _positions(qi, kb, tq, tk, pos0)
            mask = k_pos <= (q_pos | (CHUNK - 1))

        def head(h, c):
            k = k_ref[h]
            v = v_ref[h]
            for j in range(2):
                idx = 2 * h + j
                sc = _dot_nt(q_ref[idx], k)
                if use_mask:
                    sc = jnp.where(mask, sc, NEG_BIG)
                m_old = m_ref[idx]
                m_new = jnp.maximum(m_old, jnp.max(sc, axis=1, keepdims=True))
                alpha = jnp.exp2(m_old - m_new)
                p = jnp.exp2(sc - jnp.concatenate([m_new] * reps, axis=1))
                l_ref[idx] = alpha * l_ref[idx] + jnp.sum(p, axis=1, keepdims=True)
                acc_ref[idx] = alpha * acc_ref[idx] + _dot(p.astype(BF16), v)
                m_ref[idx] = m_new
            return c

        lax.fori_loop(0, DF_HEADS, head, 0, unroll=True)

    @pl.when(masked_ref[s] == 1)
    def _():
        tile(True)

    @pl.when(masked_ref[s] == 0)
    def _():
        tile(False)

    @pl.when(last_ref[s] == 1)
    def _():
        lv = lam_ref[...]
        lam = (jnp.exp(jnp.sum(lv[0:1] * lv[1:2], axis=1, keepdims=True))
               - jnp.exp(jnp.sum(lv[2:3] * lv[3:4], axis=1, keepdims=True)) + lam_init)
        gain = g_ref[...] * (1.0 - lam_init)
        for h in range(DF_HEADS):
            o = acc_ref[2 * h] / l_ref[2 * h] - lam * (acc_ref[2 * h + 1] / l_ref[2 * h + 1])
            o = o * lax.rsqrt(jnp.mean(o * o, axis=-1, keepdims=True) + RMS_EPS)
            o_ref[:, h * LANES:(h + 1) * LANES] = (o * gain).astype(BF16)


def _df_attention(q16, k16, v16, lam_vecs, subln_g, *, tq, tk, pos0, lam_init):
    B, _, S, _ = q16.shape
    n_q = S // tq
    sched = _schedule(n_q, tq, tk, pos0, lambda q_first, k_last: k_last <= (q_first | (CHUNK - 1)))
    steps = int(sched[0].shape[0])
    qspec = pl.BlockSpec((None, 2 * DF_HEADS, tq, LANES), lambda b, s, qi, kb, f, l, m: (b, 0, qi[s], 0))
    kspec = pl.BlockSpec((None, DF_HEADS, tk, LANES), lambda b, s, qi, kb, f, l, m: (b, 0, kb[s], 0))
    const = lambda b, s, qi, kb, f, l, m: (0, 0)
    grid_spec = pltpu.PrefetchScalarGridSpec(
        num_scalar_prefetch=5,
        grid=(B, steps),
        in_specs=[qspec, kspec, kspec,
                  pl.BlockSpec((4, DF_DIM), const), pl.BlockSpec((1, LANES), const)],
        out_specs=pl.BlockSpec((tq, DF_W), lambda b, s, qi, kb, f, l, m: (b * n_q + qi[s], 0)),
        scratch_shapes=[pltpu.VMEM((2 * DF_HEADS, tq, LANES), F32),
                        pltpu.VMEM((2 * DF_HEADS, tq, LANES), F32),
                        pltpu.VMEM((2 * DF_HEADS, tq, LANES), F32)],
    )
    return pl.pallas_call(
        functools.partial(_df_kernel, tq=tq, tk=tk, pos0=pos0, lam_init=lam_init),
        grid_spec=grid_spec,
        out_shape=jax.ShapeDtypeStruct((B * S, DF_W), BF16),
        compiler_params=_params(("parallel", "arbitrary")),
        name="differential_attention",
    )(*sched, q16, k16, v16, lam_vecs, subln_g.reshape(1, LANES))


def _merge_kernel(x_ref, oa_ref, ob_ref, sga_ref, sgb_ref, wa_ref, wb_ref, wo_ref, g_ref, b_ref, o_ref):
    ya = _dot(oa_ref[...], wa_ref[...])
    yb = _dot(ob_ref[...], wb_ref[...])
    mix = (sga_ref[...] * ya + sgb_ref[...] * yb).astype(BF16)
    h = DN_ALPHA * x_ref[...] + _dot(mix, wo_ref[...])
    o_ref[...] = _layer_norm(h, g_ref[...], b_ref[...])


def _merge(xf, oa, ob, sga, sgb, wa16, wb16, wo16, g, b):
    N, D = xf.shape
    tm = 256
    row = lambda i: (i, 0)
    const = lambda i: (0, 0)
    return pl.pallas_call(
        _merge_kernel,
        grid=(N // tm,),
        in_specs=[pl.BlockSpec((tm, D), row), pl.BlockSpec((tm, SB_W), row), pl.BlockSpec((tm, DF_W), row),
                  pl.BlockSpec((tm, D), row), pl.BlockSpec((tm, D), row),
                  pl.BlockSpec((SB_W, D), const), pl.BlockSpec((DF_W, D), const), pl.BlockSpec((D, D), const),
                  pl.BlockSpec((1, D), const), pl.BlockSpec((1, D), const)],
        out_specs=pl.BlockSpec((tm, D), row),
        out_shape=jax.ShapeDtypeStruct((N, D), F32),
        compiler_params=_params(("parallel",)),
        name="merge_ln1",
    )(xf, oa, ob, sga, sgb, wa16, wb16, wo16, g.reshape(1, D), b.reshape(1, D))


def _proj_kernel(x_ref, w_ref, o_ref):
    o_ref[...] = _dot(x_ref[...].astype(BF16), w_ref[...])


def _project(xf, w16):
    N, D = xf.shape
    tm = 256
    return pl.pallas_call(
        _proj_kernel,
        grid=(N // tm,),
        in_specs=[pl.BlockSpec((tm, D), lambda i: (i, 0)), pl.BlockSpec(w16.shape, lambda i: (0, 0))],
        out_specs=pl.BlockSpec((tm, w16.shape[1]), lambda i: (i, 0)),
        out_shape=jax.ShapeDtypeStruct((N, w16.shape[1]), F32),
        compiler_params=_params(("parallel",)),
        name="memory_projection",
    )(xf, w16)


def _memattn_kernel(x_ref, mk_ref, mv_ref, wq_ref, wo_ref, g_ref, b_ref, *rest):
    o_ref = rest[-1]
    x = x_ref[...]
    q = (_dot(x.astype(BF16), wq_ref[...]) * (MEM_DIM ** -0.5)).astype(BF16)
    outs = []
    for h in range(MEM_HEADS):
        sl = slice(h * MEM_DIM, (h + 1) * MEM_DIM)
        sc = _dot_nt(q[:, sl], mk_ref[:, sl].astype(BF16))
        sc = sc - jnp.max(sc, axis=-1, keepdims=True)
        p = jnp.exp(sc)
        p = p / jnp.sum(p, axis=-1, keepdims=True)
        outs.append(_dot(p.astype(BF16), mv_ref[:, sl].astype(BF16)).astype(BF16))
    o = jnp.concatenate(outs, axis=1)
    h2 = DN_ALPHA * x + _dot(o, wo_ref[...])
    o_ref[...] = _layer_norm(h2, g_ref[...], b_ref[...])


def _memory_attention(x1, mem_k, mem_v, wq16, wo16, g, b, *, batch, seq, out_rows, row_offset, prev=None):
    N, D = x1.shape
    M = mem_k.shape[1]
    tm = min(256, seq)
    spb = seq // tm
    off = row_offset // tm
    const = lambda bi, i: (0, 0)
    in_specs = [pl.BlockSpec((tm, D), lambda bi, i: (bi * spb + i, 0)),
                pl.BlockSpec((None, M, D), lambda bi, i: (bi, 0, 0)),
                pl.BlockSpec((None, M, D), lambda bi, i: (bi, 0, 0)),
                pl.BlockSpec((D, D), const), pl.BlockSpec((D, D), const),
                pl.BlockSpec((1, D), const), pl.BlockSpec((1, D), const)]
    args = [x1, mem_k, mem_v, wq16, wo16, g.reshape(1, D), b.reshape(1, D)]
    aliases = {}
    if prev is not None:
        in_specs.append(pl.BlockSpec(memory_space=pl.ANY))
        args.append(prev)
        aliases = {len(args) - 1: 0}
    return pl.pallas_call(
        _memattn_kernel,
        grid=(batch, spb),
        in_specs=in_specs,
        out_specs=pl.BlockSpec((tm, D), lambda bi, i: (off + bi * spb + i, 0)),
        out_shape=jax.ShapeDtypeStruct((out_rows, D), F32),
        input_output_aliases=aliases,
        compiler_params=_params(("parallel", "arbitrary")),
        name="memory_attention_ln2",
    )(*args)


def _router_kernel(x_ref, w_ref, b_ref, tri_ref, o_ref, cnt_ref, run_ref):
    @pl.when(pl.program_id(0) == 0)
    def _():
        run_ref[...] = jnp.zeros(run_ref.shape, F32)

    x = x_ref[...]
    tm = x.shape[0]
    x_hi = x.astype(BF16)
    x_lo = (x - x_hi.astype(F32)).astype(BF16)
    w = w_ref[...]
    w_hi = w.astype(BF16)
    w_lo = (w - w_hi.astype(F32)).astype(BF16)
    logits = _dot(x_hi, w_hi) + _dot(x_lo, w_hi) + _dot(x_hi, w_lo) + b_ref[...]
    lane = lax.broadcasted_iota(jnp.int32, (tm, LANES), 1)
    big = jnp.int32(LANES)
    glog = jnp.where(lane < N_GROUPS, logits, -jnp.inf)
    gmax = jnp.max(glog, axis=-1, keepdims=True)
    grp = jnp.min(jnp.where(glog == gmax, lane, big), axis=-1, keepdims=True)
    g_prob = 1.0 / jnp.sum(jnp.exp(glog - gmax), axis=-1, keepdims=True)
    lo = N_GROUPS + grp * EXPERTS_PER_GROUP
    in_grp = (lane >= lo) & (lane < lo + EXPERTS_PER_GROUP)
    el = jnp.where(in_grp, logits, -jnp.inf)
    v1 = jnp.max(el, axis=-1, keepdims=True)
    i1 = jnp.min(jnp.where(el == v1, lane, big), axis=-1, keepdims=True)
    el2 = jnp.where(lane == i1, -jnp.inf, el)
    v2 = jnp.max(el2, axis=-1, keepdims=True)
    i2 = jnp.min(jnp.where(el2 == v2, lane, big), axis=-1, keepdims=True)
    e2 = jnp.exp(v2 - v1)
    den = 1.0 / (1.0 + e2)
    g1 = den * g_prob
    g2 = e2 * den * g_prob
    hit1 = lane == i1
    hit2 = lane == i2
    member = jnp.where(hit1 | hit2, 1.0, 0.0)
    before = _dot(tri_ref[...], member.astype(BF16)) + run_ref[...]
    r1 = jnp.sum(jnp.where(hit1, before, 0.0), axis=-1, keepdims=True)
    r2 = jnp.sum(jnp.where(hit2, before, 0.0), axis=-1, keepdims=True)
    run = run_ref[...] + jnp.sum(member, axis=0, keepdims=True)
    run_ref[...] = run
    cnt_ref[...] = run
    out = jnp.where(lane == 0, (i1 - N_GROUPS).astype(F32), 0.0)
    out = jnp.where(lane == 1, (i2 - N_GROUPS).astype(F32), out)
    out = jnp.where(lane == 2, g1, out)
    out = jnp.where(lane == 3, g2, out)
    out = jnp.where(lane == 4, r1, out)
    out = jnp.where(lane == 5, r2, out)
    o_ref[...] = out


def _router(x2, w_group, b_group, w_router, b_router):
    N, D = x2.shape
    tm = 256
    w = jnp.zeros((D, LANES), F32).at[:, :N_GROUPS].set(w_group).at[:, N_GROUPS:N_GROUPS + N_EXPERTS].set(w_router)
    b = jnp.zeros((1, LANES), F32).at[0, :N_GROUPS].set(b_group).at[0, N_GROUPS:N_GROUPS + N_EXPERTS].set(b_router)
    r_idx = np.arange(tm)
    strict_lower = jnp.asarray((r_idx[:, None] > r_idx[None, :]).astype(np.float32)).astype(BF16)
    r, cnt = pl.pallas_call(
        _router_kernel,
        grid=(N // tm,),
        in_specs=[pl.BlockSpec((tm, D), lambda i: (i, 0)), pl.BlockSpec((D, LANES), lambda i: (0, 0)),
                  pl.BlockSpec((1, LANES), lambda i: (0, 0)), pl.BlockSpec((tm, tm), lambda i: (0, 0))],
        out_specs=[pl.BlockSpec((tm, LANES), lambda i: (i, 0)), pl.BlockSpec((1, LANES), lambda i: (0, 0))],
        out_shape=[jax.ShapeDtypeStruct((N, LANES), F32), jax.ShapeDtypeStruct((1, LANES), F32)],
        scratch_shapes=[pltpu.VMEM((1, LANES), F32)],
        compiler_params=_params(("arbitrary",)),
        name="router",
    )(x2, w, b, strict_lower)
    expert = r[:, 0:2].astype(jnp.int32)
    gate = r[:, 2:4]
    rank = r[:, 4:6].astype(jnp.int32)
    counts = cnt[0, N_GROUPS:N_GROUPS + N_EXPERTS].astype(jnp.int32)
    return expert, gate, rank, counts


def _row_scatter(idx_ref, k, src_ref, dst_hbm, sem, rows):
    for r in range(rows):
        p = idx_ref[0, k, r]
        pltpu.make_async_copy(src_ref.at[pl.ds(r, 1)], dst_hbm.at[pl.ds(p, 1)], sem).start(priority=r % 2)


def _dispatch_kernel(pos_ref, x_ref, xs_in_hbm, xs_hbm, buf_ref, sem_ref, *, tm, n_tiles):
    del xs_in_hbm
    i = pl.program_id(0)
    slot = i % 2

    def wait(sl):
        for k in range(TOP_K):
            pltpu.make_async_copy(buf_ref.at[sl], xs_hbm.at[pl.ds(0, tm)], sem_ref.at[sl]).wait()

    @pl.when(i >= 2)
    def _():
        wait(slot)

    buf_ref[slot] = x_ref[...]
    for k in range(TOP_K):
        _row_scatter(pos_ref, k, buf_ref.at[slot], xs_hbm, sem_ref.at[slot], tm)

    @pl.when(i == n_tiles - 1)
    def _():
        wait(slot)
        if n_tiles > 1:
            wait(1 - slot)


def _dispatch(x2, pos3, rows):
    N, D = x2.shape
    nt, _, tm = pos3.shape
    xs0 = jnp.zeros((rows, D), F32)
    return pl.pallas_call(
        functools.partial(_dispatch_kernel, tm=tm, n_tiles=nt),
        grid=(nt,),
        in_specs=[pl.BlockSpec((1, TOP_K, tm), lambda i: (i, 0, 0), memory_space=pltpu.SMEM),
                  pl.BlockSpec((tm, D), lambda i: (i, 0)),
                  pl.BlockSpec(memory_space=pl.ANY)],
        out_specs=pl.BlockSpec(memory_space=pl.ANY),
        out_shape=jax.ShapeDtypeStruct((rows, D), F32),
        scratch_shapes=[pltpu.VMEM((2, tm, D), F32), pltpu.SemaphoreType.DMA((2,))],
        input_output_aliases={2: 0},
        compiler_params=_params(("arbitrary",)),
        name="expert_dispatch",
    )(pos3, x2, xs0)


def _row_gather(idx_ref, k, src_hbm, dst_ref, sem, rows):
    for r in range(rows):
        t = idx_ref[0, k, r]
        pltpu.make_async_copy(src_hbm.at[pl.ds(t, 1)], dst_ref.at[pl.ds(r, 1)], sem).start(priority=r % 2)


def _gather_wait(src_hbm, dst_ref, sem, rows):
    pltpu.make_async_copy(src_hbm.at[pl.ds(0, rows)], dst_ref, sem).wait()


def _expert_kernel(be_ref, used_ref, x_ref, wg_ref, wu_ref, wd_ref, o_ref):
    i = pl.program_id(0)

    @pl.when(i < used_ref[0])
    def _():
        xb = x_ref[...].astype(BF16)
        hg = _dot(xb, wg_ref[...])
        hu = _dot(xb, wu_ref[...])
        hid = (hg * jax.nn.sigmoid(hg) * hu).astype(BF16)
        o_ref[...] = _dot(hid, wd_ref[...])

    @pl.when(i >= used_ref[0])
    def _():
        o_ref[...] = jnp.zeros(o_ref.shape, F32)


def _experts(xs, blk_e, n_used, wg16, wu16, wd16):
    R, D = xs.shape
    nb = R // EBLOCK
    grid_spec = pltpu.PrefetchScalarGridSpec(
        num_scalar_prefetch=2,
        grid=(nb,),
        in_specs=[pl.BlockSpec((EBLOCK, D), lambda i, be, nu: (i, 0)),
                  pl.BlockSpec((None, D, D_EXPERT), lambda i, be, nu: (be[i], 0, 0)),
                  pl.BlockSpec((None, D, D_EXPERT), lambda i, be, nu: (be[i], 0, 0)),
                  pl.BlockSpec((None, D_EXPERT, D), lambda i, be, nu: (be[i], 0, 0))],
        out_specs=pl.BlockSpec((EBLOCK, D), lambda i, be, nu: (i, 0)),
    )
    return pl.pallas_call(
        _expert_kernel,
        grid_spec=grid_spec,
        out_shape=jax.ShapeDtypeStruct((R, D), F32),
        compiler_params=_params(("arbitrary",)),
        name="grouped_experts",
    )(blk_e, n_used, xs, wg16, wu16, wd16)


def _combine_kernel(pos_cur_ref, pos_next_ref, x_ref, gate_ref, g_ref, b_ref, y_hbm, oa_ref, ob_ref,
                    buf_ref, sem_ref, *, tm, n_tiles, n_tiles_a):
    i = pl.program_id(0)
    slot = i % 2

    def gather(pos_ref, sl):
        for k in range(TOP_K):
            _row_gather(pos_ref, k, y_hbm, buf_ref.at[sl, k], sem_ref.at[sl], tm)

    def wait(sl):
        for k in range(TOP_K):
            _gather_wait(y_hbm, buf_ref.at[sl, k], sem_ref.at[sl], tm)

    @pl.when(i == 0)
    def _():
        gather(pos_cur_ref, 0)

    wait(slot)
    gather(pos_next_ref, 1 - slot)
    gate = gate_ref[...]
    moe = buf_ref[slot, 0] * gate[:, 0:1] + buf_ref[slot, 1] * gate[:, 1:2]
    h = DN_ALPHA * x_ref[...] + moe
    res = _layer_norm(h, g_ref[...], b_ref[...])

    @pl.when(i < n_tiles_a)
    def _():
        oa_ref[...] = res

    @pl.when(i >= n_tiles_a)
    def _():
        ob_ref[...] = res

    @pl.when(i == n_tiles - 1)
    def _():
        wait(1 - slot)


def _combine(x2, ys, pos3, gate, g, b, rows_a):
    N, D = x2.shape
    nt, _, tm = pos3.shape
    nta = rows_a // tm
    gate_p = jnp.zeros((N, LANES), F32).at[:, 0:TOP_K].set(gate)
    return pl.pallas_call(
        functools.partial(_combine_kernel, tm=tm, n_tiles=nt, n_tiles_a=nta),
        grid=(nt,),
        in_specs=[pl.BlockSpec((1, TOP_K, tm), lambda i: (i, 0, 0), memory_space=pltpu.SMEM),
                  pl.BlockSpec((1, TOP_K, tm), lambda i: (jnp.minimum(i + 1, nt - 1), 0, 0),
                               memory_space=pltpu.SMEM),
                  pl.BlockSpec((tm, D), lambda i: (i, 0)),
                  pl.BlockSpec((tm, LANES), lambda i: (i, 0)),
                  pl.BlockSpec((1, D), lambda i: (0, 0)), pl.BlockSpec((1, D), lambda i: (0, 0)),
                  pl.BlockSpec(memory_space=pl.ANY)],
        out_specs=[pl.BlockSpec((tm, D), lambda i: (jnp.minimum(i, nta - 1), 0)),
                   pl.BlockSpec((tm, D), lambda i: (jnp.maximum(i - nta, 0), 0))],
        out_shape=[jax.ShapeDtypeStruct((rows_a, D), F32), jax.ShapeDtypeStruct((N - rows_a, D), F32)],
        scratch_shapes=[pltpu.VMEM((2, TOP_K, tm, D), F32), pltpu.SemaphoreType.DMA((2,))],
        compiler_params=_params(("arbitrary",)),
        name="combine_ln3",
    )(pos3, pos3, x2, gate_p, g.reshape(1, D), b.reshape(1, D), ys)


def _dispatch_plan(expert, rank, counts):
    N, K = expert.shape
    A = N * K
    E = N_EXPERTS
    padded = (counts + EBLOCK - 1) // EBLOCK * EBLOCK
    pad_end = jnp.cumsum(padded)
    pad_start = pad_end - padded
    n_blocks = (A + E * (EBLOCK - 1) + EBLOCK - 1) // EBLOCK
    ids = jnp.arange(E, dtype=jnp.int32)
    pos = (jnp.sum(jnp.where(expert[..., None] == ids, pad_start, 0), axis=-1) + rank).astype(jnp.int32)
    blk_start = jnp.arange(n_blocks, dtype=jnp.int32) * EBLOCK
    blk_e = jnp.minimum(jnp.sum((pad_end[None, :] <= blk_start[:, None]).astype(jnp.int32), axis=1), E - 1)
    n_used = (pad_end[-1] // EBLOCK).astype(jnp.int32).reshape(1)
    return pos, blk_e.astype(jnp.int32), n_used, n_blocks * EBLOCK


def _hier_moe_ln3(x2, rows_a, w_group, b_group, w_router, b_router, wg16, wu16, wd16, g, b):
    N = x2.shape[0]
    tm = 256
    expert, gate, rank, counts = _router(x2, w_group, b_group, w_router, b_router)
    pos, blk_e, n_used, rows = _dispatch_plan(expert, rank, counts)
    pos3 = pos.reshape(N // tm, tm, TOP_K).transpose(0, 2, 1)
    xs = _dispatch(x2, pos3, rows)
    ys = _experts(xs, blk_e, n_used, wg16, wu16, wd16)
    return _combine(x2, ys, pos3, gate, g, b, rows_a)


def _mixers(x, caches, lam_init, w):
    B, S, D = x.shape
    N = B * S
    xf = x.reshape(N, D)
    past = 0 if caches is None else caches[0].shape[1]
    (ka, va, kb, vb, sga, sgb, qa16, ka16, va16, qb16, kb16, vb16) = _in_projection(x, w['w_in'], past)
    if caches is None:
        tq = min(512, S)
        tk = min(256, S)
        keys = (ka16, va16, kb16, vb16)
    else:
        tq = S
        tk = 256
        views = (caches[0].reshape(B, past, -1), caches[1].reshape(B, past, -1),
                 caches[2].reshape(B, past * DF_HEADS, LANES), caches[3].reshape(B, past * DF_HEADS, LANES))
        keys = tuple(_pack_keys(c, n, past + tk) for c, n in zip(views, (ka16, va16, kb16, vb16)))
    oa = _sb_attention(qa16, keys[0], keys[1], tq=tq, tk=tk, pos0=past)
    tk_df = past + tk if caches is not None else min(1024, S)
    ob = _df_attention(qb16, keys[2], keys[3], w['lam'], w['subln_g'], tq=tq, tk=tk_df, pos0=past,
                       lam_init=lam_init)
    x1 = _merge(xf, oa, ob, sga, sgb, w['w_branch_a'], w['w_branch_b'], w['w_out'], w['ln1_g'], w['ln1_b'])
    new_rows = (ka.reshape(B, S, SB_HEADS, SB_DIM), va.reshape(B, S, SB_HEADS, SB_DIM),
                kb.reshape(B, S, DF_HEADS, 2 * DF_DIM), vb.reshape(B, S, DF_HEADS, 2 * DF_DIM))
    return x1, new_rows


def kernel(x_prompt, mem_prompt, x_sample, cache_sb_k, cache_sb_v, cache_diff_k, cache_diff_v, cache_mem_k, cache_mem_v, w_in, lam_q1, lam_k1, lam_q2, lam_k2, subln_g, w_branch_a, w_branch_b, w_out, ln1_g, ln1_b, w_mq, w_mk, w_mv, w_mo, ln2_g, ln2_b, w_group, b_group, w_router, b_router, w_up, w_gate, w_down, ln3_g, ln3_b):
    depth = w_in.shape[0]
    yp, ys = x_prompt, x_sample
    Bp, n_mem = mem_prompt.shape[0], mem_prompt.shape[1]
    Bs = x_sample.shape[0]
    outs = [[] for _ in range(10)]
    for l in range(depth):
        lam_init = 0.8 - 0.6 * math.exp(-0.3 * l)
        w = {
            'w_in': w_in[l].astype(BF16),
            'lam': jnp.stack([lam_q1[l], lam_k1[l], lam_q2[l], lam_k2[l]]).astype(F32),
            'subln_g': subln_g[l],
            'w_branch_a': w_branch_a[l].astype(BF16), 'w_branch_b': w_branch_b[l].astype(BF16),
            'w_out': w_out[l].astype(BF16), 'ln1_g': ln1_g[l], 'ln1_b': ln1_b[l],
            'w_mq': w_mq[l].astype(BF16), 'w_mo': w_mo[l].astype(BF16),
            'ln2_g': ln2_g[l], 'ln2_b': ln2_b[l],
            'w_group': w_group[l], 'b_group': b_group[l], 'w_router': w_router[l], 'b_router': b_router[l],
            'w_gate': w_gate[l].astype(BF16), 'w_up': w_up[l].astype(BF16), 'w_down': w_down[l].astype(BF16),
            'ln3_g': ln3_g[l], 'ln3_b': ln3_b[l],
        }
        memf = mem_prompt.reshape(Bp * n_mem, D_MODEL)
        mk = _project(memf, w_mk[l].astype(BF16)).reshape(Bp, n_mem, D_MODEL)
        mv = _project(memf, w_mv[l].astype(BF16)).reshape(Bp, n_mem, D_MODEL)
        Np = Bp * yp.shape[1]
        Ns = Bs * ys.shape[1]
        x1p, rows_p = _mixers(yp, None, lam_init, w)
        for lst, val in zip(outs[0:6], rows_p + (mk.reshape(Bp, n_mem, MEM_HEADS, MEM_DIM),
                                               mv.reshape(Bp, n_mem, MEM_HEADS, MEM_DIM))):
            lst.append(val)
        caches = (cache_sb_k[l], cache_sb_v[l], cache_diff_k[l], cache_diff_v[l])
        x1s, rows_s = _mixers(ys, caches, lam_init, w)
        for lst, val in zip(outs[6:10], rows_s):
            lst.append(val)
        x2 = _memory_attention(x1p, mk, mv, w['w_mq'], w['w_mo'], w['ln2_g'], w['ln2_b'],
                               batch=Bp, seq=yp.shape[1], out_rows=Np + Ns, row_offset=0,
                               prev=jnp.zeros((Np + Ns, D_MODEL), F32))
        x2 = _memory_attention(x1s, cache_mem_k[l].reshape(Bs, n_mem, D_MODEL),
                               cache_mem_v[l].reshape(Bs, n_mem, D_MODEL),
                               w['w_mq'], w['w_mo'], w['ln2_g'], w['ln2_b'],
                               batch=Bs, seq=ys.shape[1], out_rows=Np + Ns, row_offset=Np, prev=x2)
        x3p, x3s = _hier_moe_ln3(x2, Np, w['w_group'], w['b_group'], w['w_router'], w['b_router'],
                                 w['w_gate'], w['w_up'], w['w_down'], w['ln3_g'], w['ln3_b'])
        yp = x3p.reshape(yp.shape)
        ys = x3s.reshape(ys.shape)
    return (yp, ys) + tuple(jnp.stack(o) for o in outs)
```

```python
import functools
import math

import jax
import jax.numpy as jnp
import numpy as np
from jax import lax
from jax.experimental import pallas as pl
from jax.experimental.pallas import tpu as pltpu

F32 = jnp.float32
BF16 = jnp.bfloat16

D_MODEL = 1024
SB_HEADS = 8
SB_DIM = 64
SB_W = SB_HEADS * SB_DIM
DF_HEADS = 4
DF_DIM = 64
DF_W = DF_HEADS * 2 * DF_DIM
CHUNK = 64
MEM_HEADS = 4
MEM_DIM = D_MODEL // MEM_HEADS
N_GROUPS = 4
EXPERTS_PER_GROUP = 8
N_EXPERTS = N_GROUPS * EXPERTS_PER_GROUP
TOP_K = 2
D_EXPERT = 512
EBLOCK = 256
ROPE_THETA = 10000.0
LN_EPS = 1e-5
RMS_EPS = 1e-5
DEPTH = 1
DN_ALPHA = (2 * DEPTH) ** 0.25
IN_W = 3 * SB_W + 3 * DF_W + 2 * D_MODEL

LANES = 128
HEAD_GROUPS = 4
LOG2E = 1.4426950408889634
NEG_BIG = -1e30
VMEM_LIMIT = 48 * 1024 * 1024


def _dot(a, b):
    return jnp.dot(a, b, preferred_element_type=F32)


def _dot_nt(a, b):
    return lax.dot_general(a, b, (((1,), (1,)), ((), ())), preferred_element_type=F32)


def _layer_norm(h, g, b):
    mu = jnp.mean(h, axis=-1, keepdims=True)
    d = h - mu
    var = jnp.mean(d * d, axis=-1, keepdims=True)
    return d * lax.rsqrt(var + LN_EPS) * g + b


def _params(sem, vmem=VMEM_LIMIT):
    return pltpu.CompilerParams(dimension_semantics=sem, vmem_limit_bytes=vmem)


def _inproj_kernel(x_ref, w_ref, cos_ref, sina_ref, sinb_ref,
                   ka_ref, va_ref, kb_ref, vb_ref, sga_ref, sgb_ref,
                   qa16_ref, ka16_ref, va16_ref, qb16_ref, kb16_ref, vb16_ref):
    xb = x_ref[...].astype(BF16)

    def mm(off, width):
        return _dot(xb, w_ref[:, off:off + width])

    def put_groups(ref, val):
        for g in range(HEAD_GROUPS):
            ref[g] = val[:, g * LANES:(g + 1) * LANES].astype(BF16)

    lane = lax.broadcasted_iota(jnp.int32, (1, LANES), 1)
    keep = (jnp.where(lane < LANES // 2, 1.0, 0.0), jnp.where(lane >= LANES // 2, 1.0, 0.0))

    def put_halves(ref, val):
        for g in range(HEAD_GROUPS):
            slab = val[:, g * LANES:(g + 1) * LANES]
            for j in range(2):
                ref[2 * g + j] = (slab * keep[j]).astype(BF16)

    def put_head_rows(ref, val):
        for h in range(DF_HEADS):
            ref[pl.ds(h, val.shape[0], stride=DF_HEADS), :] = val[:, h * LANES:(h + 1) * LANES]

    def rope(t):
        n = t.shape[1]
        reps = n // LANES
        cos = jnp.concatenate([cos_ref[...]] * reps, axis=1)
        sina = jnp.concatenate([sina_ref[...]] * reps, axis=1)
        sinb = jnp.concatenate([sinb_ref[...]] * reps, axis=1)
        fwd = pltpu.roll(t, n - DF_DIM // 2, 1)
        bwd = pltpu.roll(t, DF_DIM // 2, 1)
        return t * cos + fwd * sina + bwd * sinb

    qa = mm(0, SB_W)
    put_halves(qa16_ref, qa * (LOG2E * SB_DIM ** -0.5))
    ka = mm(SB_W, SB_W)
    ka_ref[...] = ka
    put_groups(ka16_ref, ka)
    va = mm(2 * SB_W, SB_W)
    va_ref[...] = va
    put_halves(va16_ref, va)
    off = 3 * SB_W
    qb = rope(mm(off, DF_W))
    put_halves(qb16_ref, qb * (LOG2E * DF_DIM ** -0.5))
    kb = rope(mm(off + DF_W, DF_W))
    put_head_rows(kb_ref, kb)
    put_groups(kb16_ref, kb)
    vb = mm(off + 2 * DF_W, DF_W)
    put_head_rows(vb_ref, vb)
    put_groups(vb16_ref, vb)
    off = off + 3 * DF_W
    sga_ref[...] = jax.nn.sigmoid(mm(off, D_MODEL))
    sgb_ref[...] = jax.nn.sigmoid(mm(off + D_MODEL, D_MODEL))


def _rope_tables(pos0, seq):
    half = DF_DIM // 2
    inv_freq = jnp.exp(jnp.arange(half, dtype=F32) * (-math.log(ROPE_THETA) / half))
    ang = (pos0 + jnp.arange(seq, dtype=jnp.int32)).astype(F32)[:, None] * inv_freq[None, :]
    cos, sin = jnp.cos(ang), jnp.sin(ang)
    zero = jnp.zeros_like(sin)
    reps = LANES // DF_DIM
    cos_t = jnp.tile(jnp.concatenate([cos, cos], axis=1), (1, reps))
    sina_t = jnp.tile(jnp.concatenate([-sin, zero], axis=1), (1, reps))
    sinb_t = jnp.tile(jnp.concatenate([zero, sin], axis=1), (1, reps))
    return cos_t, sina_t, sinb_t


def _in_projection(x, w_in16, pos0):
    B, S, D = x.shape
    N = B * S
    tm = min(256, S)
    spb = S // tm
    xf = x.reshape(N, D)
    cos_t, sina_t, sinb_t = _rope_tables(pos0, S)
    row = lambda i: (i, 0)
    tab = lambda i: (i % spb, 0)
    grp = lambda i: (i // spb, 0, i % spb, 0)
    f32_512 = jax.ShapeDtypeStruct((N, SB_W), F32)
    f32_1024 = jax.ShapeDtypeStruct((N, D), F32)
    g16 = jax.ShapeDtypeStruct((B, HEAD_GROUPS, S, LANES), BF16)
    h16 = jax.ShapeDtypeStruct((B, 2 * HEAD_GROUPS, S, LANES), BF16)
    f32_rows = jax.ShapeDtypeStruct((N * DF_HEADS, LANES), F32)
    rspec = pl.BlockSpec((tm * DF_HEADS, LANES), row)
    spec512 = pl.BlockSpec((tm, SB_W), row)
    spec1024 = pl.BlockSpec((tm, D), row)
    gspec = pl.BlockSpec((None, HEAD_GROUPS, tm, LANES), grp)
    hspec = pl.BlockSpec((None, 2 * HEAD_GROUPS, tm, LANES), grp)
    tspec = pl.BlockSpec((tm, LANES), tab)
    return pl.pallas_call(
        _inproj_kernel,
        grid=(N // tm,),
        in_specs=[spec1024, pl.BlockSpec((D, IN_W), lambda i: (0, 0)), tspec, tspec, tspec],
        out_specs=[spec512, spec512, rspec, rspec, spec1024, spec1024,
                   hspec, gspec, hspec, hspec, gspec, gspec],
        out_shape=[f32_512, f32_512, f32_rows, f32_rows, f32_1024, f32_1024,
                   h16, g16, h16, h16, g16, g16],
        compiler_params=_params(("parallel",), 56 * 1024 * 1024),
        name="in_projection",
    )(xf, w_in16, cos_t, sina_t, sinb_t)


def _pack_kernel(c_ref, n_ref, o_ref, *, past, new_rows, halves, head_rows):
    n_slabs = o_ref.shape[0]
    if head_rows:
        for h in range(n_slabs):
            o_ref[h, 0:past, :] = c_ref[pl.ds(h, past, stride=n_slabs), :].astype(BF16)
    else:
        c = c_ref[...]
        lane = lax.broadcasted_iota(jnp.int32, (1, LANES), 1)
        keep = (jnp.where(lane < LANES // 2, 1.0, 0.0), jnp.where(lane >= LANES // 2, 1.0, 0.0))
        for g in range(HEAD_GROUPS):
            slab = c[:, g * LANES:(g + 1) * LANES]
            if halves:
                for j in range(2):
                    o_ref[2 * g + j, 0:past, :] = (slab * keep[j]).astype(BF16)
            else:
                o_ref[g, 0:past, :] = slab.astype(BF16)
    pad = o_ref.shape[1] - past - new_rows
    o_ref[:, past:past + new_rows, :] = n_ref[...]
    o_ref[:, past + new_rows:, :] = jnp.zeros((n_slabs, pad, LANES), BF16)


def _pack_keys(cache, new16, total):
    B, R, W = cache.shape
    n, S = new16.shape[1], new16.shape[2]
    head_rows = W == LANES
    P = R // n if head_rows else R
    return pl.pallas_call(
        functools.partial(_pack_kernel, past=P, new_rows=S, halves=(n == 2 * HEAD_GROUPS), head_rows=head_rows),
        grid=(B,),
        in_specs=[pl.BlockSpec((None, R, W), lambda b: (b, 0, 0)),
                  pl.BlockSpec((None, n, S, LANES), lambda b: (b, 0, 0, 0))],
        out_specs=pl.BlockSpec((None, n, total, LANES), lambda b: (b, 0, 0, 0)),
        out_shape=jax.ShapeDtypeStruct((B, n, total, LANES), BF16),
        compiler_params=_params(("parallel",)),
        name="pack_keys",
    )(cache, new16)


def _schedule(n_q, tq, tk, pos0, variant):
    qi, kb, first, last, masked = [], [], [], [], []
    for i in range(n_q):
        top = -(-(pos0 + (i + 1) * tq) // tk) - 1
        for j in range(top, -1, -1):
            qi.append(i)
            kb.append(j)
            first.append(1 if j == top else 0)
            last.append(1 if j == 0 else 0)
            masked.append(variant(pos0 + i * tq, j * tk, (j + 1) * tk - 1))
    mk = lambda v: jnp.asarray(np.asarray(v, np.int32))
    return mk(qi), mk(kb), mk(first), mk(last), mk(masked)


def _positions(qi, kb, tq, tk, pos0):
    q_pos = pos0 + qi * tq + lax.broadcasted_iota(jnp.int32, (tq, 1), 0)
    k_pos = kb * tk + lax.broadcasted_iota(jnp.int32, (1, tk), 1)
    return q_pos, k_pos


SOFTPLUS_CLAMP = 64.0
STACKED_ROWS_MAX = 512
BF16_BITS = 0xFFFF0000


def _sb_kernel(qi_ref, kb_ref, first_ref, last_ref, masked_ref, q_ref, k_ref, v_ref, tri_ref, o_ref,
               acc_ref, carry_ref, *, tq, tk, pos0):
    stack_heads = SB_HEADS * tq <= STACKED_ROWS_MAX
    trim = min(tk, tq)
    n_masked_variants = 1 if stack_heads else tq // trim
    s = pl.program_id(1)
    qi = qi_ref[s]
    kb = kb_ref[s]

    @pl.when(first_ref[s] == 1)
    def _():
        acc_ref[...] = jnp.zeros(acc_ref.shape, F32)
        carry_ref[...] = jnp.zeros(carry_ref.shape, F32)

    reps = tk // LANES

    def weights(y, mask):
        sp = jnp.maximum(y, jnp.log2(1.0 + jnp.exp2(jnp.minimum(y, SOFTPLUS_CLAMP))))
        if mask is not None:
            sp = jnp.where(mask, sp, 0.0)
        hi = pltpu.bitcast(pltpu.bitcast(sp, jnp.uint32) & jnp.uint32(BF16_BITS), F32)
        lo = sp - hi
        split = jnp.concatenate([hi.astype(BF16), lo.astype(BF16)], axis=1)
        return _dot(split, tri_ref[...])

    def probs(y, total, mask):
        a = jnp.exp2(y - total)
        if mask is not None:
            a = jnp.where(mask, a, 0.0)
        return a.astype(BF16)

    def tile(use_mask, row0=0):
        mask = None
        if stack_heads:
            rows = SB_HEADS * tq
            if use_mask:
                q_pos, k_pos = _positions(qi, kb, tq, tk, pos0)
                mask = jnp.concatenate([k_pos < q_pos] * SB_HEADS, axis=0)
            y = jnp.concatenate([_dot_nt(q_ref[h], k_ref[h // 2]) for h in range(SB_HEADS)], axis=0)
            suffix = weights(y, mask)
            carry = carry_ref[...].reshape(rows, LANES)
            a = probs(y, suffix + jnp.concatenate([carry] * reps, axis=1), mask)
            carry_ref[...] = (carry + jnp.broadcast_to(suffix[:, 0:1], (rows, LANES))).reshape(carry_ref.shape)
            for g in range(HEAD_GROUPS):
                upd = (_dot(a[(2 * g) * tq:(2 * g + 1) * tq], v_ref[2 * g])
                       + _dot(a[(2 * g + 1) * tq:(2 * g + 2) * tq], v_ref[2 * g + 1]))
                acc_ref[g] = acc_ref[g] + upd
            return
        n_rows = tq - row0
        live = pl.ds(row0, n_rows)
        if use_mask:
            q_pos, k_pos = _positions(qi, kb, tq, tk, pos0)
            mask = k_pos < q_pos[row0:]

        def group(g, c):
            k = k_ref[g]
            upd = jnp.zeros((n_rows, LANES), F32)
            for j in range(2):
                y = _dot_nt(q_ref[2 * g + j, live, :], k)
                suffix = weights(y, mask)
                carry = carry_ref[2 * g + j, live, :]
                a = probs(y, suffix + jnp.concatenate([carry] * reps, axis=1), mask)
                upd = upd + _dot(a, v_ref[2 * g + j])
                carry_ref[2 * g + j, live, :] = carry + jnp.broadcast_to(suffix[:, 0:1], (n_rows, LANES))
            acc_ref[g, live, :] = acc_ref[g, live, :] + upd
            return c

        lax.fori_loop(0, HEAD_GROUPS, group, 0, unroll=True)

    @pl.when(masked_ref[s] == 0)
    def _():
        tile(False)

    for v in range(n_masked_variants):
        @pl.when(masked_ref[s] == v + 1)
        def _(v=v):
            tile(True, v * trim)

    @pl.when(last_ref[s] == 1)
    def _():
        for g in range(HEAD_GROUPS):
            o_ref[:, g * LANES:(g + 1) * LANES] = acc_ref[g].astype(BF16)


def _tri_ones_stacked(tk):
    r = np.arange(tk)
    tri = (r[:, None] >= r[None, :]).astype(np.float32)
    return jnp.asarray(np.concatenate([tri, tri], axis=0)).astype(BF16)


def _sb_attention(q16, k16, v16, *, tq, tk, pos0):
    B, _, S, _ = q16.shape
    n_q = S // tq
    trim = min(tk, tq)
    stacked = SB_HEADS * tq <= STACKED_ROWS_MAX

    def variant(q_first, k_first, k_last):
        if k_last < q_first:
            return 0
        return 1 if stacked else 1 + min(max(k_first - q_first, 0) // trim, tq // trim - 1)

    sched = _schedule(n_q, tq, tk, pos0, variant)
    steps = int(sched[0].shape[0])
    qspec = pl.BlockSpec((None, SB_HEADS, tq, LANES), lambda b, s, qi, kb, f, l, m: (b, 0, qi[s], 0))
    kspec = pl.BlockSpec((None, HEAD_GROUPS, tk, LANES), lambda b, s, qi, kb, f, l, m: (b, 0, kb[s], 0))
    vspec = pl.BlockSpec((None, SB_HEADS, tk, LANES), lambda b, s, qi, kb, f, l, m: (b, 0, kb[s], 0))
    grid_spec = pltpu.PrefetchScalarGridSpec(
        num_scalar_prefetch=5,
        grid=(B, steps),
        in_specs=[qspec, kspec, vspec,
                  pl.BlockSpec((2 * tk, tk), lambda b, s, qi, kb, f, l, m: (0, 0))],
        out_specs=pl.BlockSpec((tq, SB_W), lambda b, s, qi, kb, f, l, m: (b * n_q + qi[s], 0)),
        scratch_shapes=[pltpu.VMEM((HEAD_GROUPS, tq, LANES), F32),
                        pltpu.VMEM((SB_HEADS, tq, LANES), F32)],
    )
    return pl.pallas_call(
        functools.partial(_sb_kernel, tq=tq, tk=tk, pos0=pos0),
        grid_spec=grid_spec,
        out_shape=jax.ShapeDtypeStruct((B * S, SB_W), BF16),
        compiler_params=_params(("parallel", "arbitrary")),
        name="stick_breaking_attention",
    )(*sched, q16, k16, v16, _tri_ones_stacked(tk))


def _df_kernel(qi_ref, kb_ref, first_ref, last_ref, masked_ref, q_ref, k_ref, v_ref, lam_ref, g_ref,
               o_ref, m_ref, l_ref, acc_ref, *, tq, tk, pos0, lam_init):
    s = pl.program_id(1)
    qi = qi_ref[s]
    kb = kb_ref[s]

    @pl.when(first_ref[s] == 1)
    def _():
        m_ref[...] = jnp.full(m_ref.shape, NEG_BIG, F32)
        l_ref[...] = jnp.zeros(l_ref.shape, F32)
        acc_ref[...] = jnp.zeros(acc_ref.shape, F32)

    reps = tk // LANES

    def tile(use_mask):
        if use_mask:
            q_pos, k_pos = _positions(qi, kb, tq, tk, pos0)
            mask = k_pos <= (q_pos | (CHUNK - 1))

        def head(h, c):
            k = k_ref[h]
            v = v_ref[h]
            for j in range(2):
                idx = 2 * h + j
                sc = _dot_nt(q_ref[idx], k)
                if use_mask:
                    sc = jnp.where(mask, sc, NEG_BIG)
                m_old = m_ref[idx]
                m_new = jnp.maximum(m_old, jnp.max(sc, axis=1, keepdims=True))
                alpha = jnp.exp2(m_old - m_new)
                p = jnp.exp2(sc - jnp.concatenate([m_new] * reps, axis=1))
                l_ref[idx] = alpha * l_ref[idx] + jnp.sum(p, axis=1, keepdims=True)
                acc_ref[idx] = alpha * acc_ref[idx] + _dot(p.astype(BF16), v)
                m_ref[idx] = m_new
            return c

        lax.fori_loop(0, DF_HEADS, head, 0, unroll=True)

    @pl.when(masked_ref[s] == 1)
    def _():
        tile(True)

    @pl.when(masked_ref[s] == 0)
    def _():
        tile(False)

    @pl.when(last_ref[s] == 1)
    def _():
        lv = lam_ref[...]
        lam = (jnp.exp(jnp.sum(lv[0:1] * lv[1:2], axis=1, keepdims=True))
               - jnp.exp(jnp.sum(lv[2:3] * lv[3:4], axis=1, keepdims=True)) + lam_init)
        gain = g_ref[...] * (1.0 - lam_init)
        for h in range(DF_HEADS):
            o = acc_ref[2 * h] / l_ref[2 * h] - lam * (acc_ref[2 * h + 1] / l_ref[2 * h + 1])
            o = o * lax.rsqrt(jnp.mean(o * o, axis=-1, keepdims=True) + RMS_EPS)
            o_ref[:, h * LANES:(h + 1) * LANES] = (o * gain).astype(BF16)


def _df_attention(q16, k16, v16, lam_vecs, subln_g, *, tq, tk, pos0, lam_init):
    B, _, S, _ = q16.shape
    n_q = S // tq
    sched = _schedule(n_q, tq, tk, pos0,
                      lambda q_first, k_first, k_last: 0 if k_last <= (q_first | (CHUNK - 1)) else 1)
    steps = int(sched[0].shape[0])
    qspec = pl.BlockSpec((None, 2 * DF_HEADS, tq, LANES), lambda b, s, qi, kb, f, l, m: (b, 0, qi[s], 0))
    kspec = pl.BlockSpec((None, DF_HEADS, tk, LANES), lambda b, s, qi, kb, f, l, m: (b, 0, kb[s], 0))
    const = lambda b, s, qi, kb, f, l, m: (0, 0)
    grid_spec = pltpu.PrefetchScalarGridSpec(
        num_scalar_prefetch=5,
        grid=(B, steps),
        in_specs=[qspec, kspec, kspec,
                  pl.BlockSpec((4, DF_DIM), const), pl.BlockSpec((1, LANES), const)],
        out_specs=pl.BlockSpec((tq, DF_W), lambda b, s, qi, kb, f, l, m: (b * n_q + qi[s], 0)),
        scratch_shapes=[pltpu.VMEM((2 * DF_HEADS, tq, LANES), F32),
                        pltpu.VMEM((2 * DF_HEADS, tq, LANES), F32),
                        pltpu.VMEM((2 * DF_HEADS, tq, LANES), F32)],
    )
    return pl.pallas_call(
        functools.partial(_df_kernel, tq=tq, tk=tk, pos0=pos0, lam_init=lam_init),
        grid_spec=grid_spec,
        out_shape=jax.ShapeDtypeStruct((B * S, DF_W), BF16),
        compiler_params=_params(("parallel", "arbitrary")),
        name="differential_attention",
    )(*sched, q16, k16, v16, lam_vecs, subln_g.reshape(1, LANES))


def _merge_kernel(x_ref, oa_ref, ob_ref, sga_ref, sgb_ref, wa_ref, wb_ref, wo_ref, g_ref, b_ref, o_ref):
    ya = _dot(oa_ref[...], wa_ref[...])
    yb = _dot(ob_ref[...], wb_ref[...])
    mix = (sga_ref[...] * ya + sgb_ref[...] * yb).astype(BF16)
    h = DN_ALPHA * x_ref[...] + _dot(mix, wo_ref[...])
    o_ref[...] = _layer_norm(h, g_ref[...], b_ref[...])


def _merge(xf, oa, ob, sga, sgb, wa16, wb16, wo16, g, b):
    N, D = xf.shape
    tm = 256
    row = lambda i: (i, 0)
    const = lambda i: (0, 0)
    return pl.pallas_call(
        _merge_kernel,
        grid=(N // tm,),
        in_specs=[pl.BlockSpec((tm, D), row), pl.BlockSpec((tm, SB_W), row), pl.BlockSpec((tm, DF_W), row),
                  pl.BlockSpec((tm, D), row), pl.BlockSpec((tm, D), row),
                  pl.BlockSpec((SB_W, D), const), pl.BlockSpec((DF_W, D), const), pl.BlockSpec((D, D), const),
                  pl.BlockSpec((1, D), const), pl.BlockSpec((1, D), const)],
        out_specs=pl.BlockSpec((tm, D), row),
        out_shape=jax.ShapeDtypeStruct((N, D), F32),
        compiler_params=_params(("parallel",)),
        name="merge_ln1",
    )(xf, oa, ob, sga, sgb, wa16, wb16, wo16, g.reshape(1, D), b.reshape(1, D))


def _proj_kernel(x_ref, w_ref, o_ref):
    o_ref[...] = _dot(x_ref[...].astype(BF16), w_ref[...])


def _project(xf, w16):
    N, D = xf.shape
    tm = 256
    return pl.pallas_call(
        _proj_kernel,
        grid=(N // tm,),
        in_specs=[pl.BlockSpec((tm, D), lambda i: (i, 0)), pl.BlockSpec(w16.shape, lambda i: (0, 0))],
        out_specs=pl.BlockSpec((tm, w16.shape[1]), lambda i: (i, 0)),
        out_shape=jax.ShapeDtypeStruct((N, w16.shape[1]), F32),
        compiler_params=_params(("parallel",)),
        name="memory_projection",
    )(xf, w16)


def _memattn_kernel(x_ref, mk_ref, mv_ref, wq_ref, wo_ref, g_ref, b_ref, *rest):
    o_ref = rest[-1]
    x = x_ref[...]
    q = (_dot(x.astype(BF16), wq_ref[...]) * (MEM_DIM ** -0.5)).astype(BF16)
    outs = []
    for h in range(MEM_HEADS):
        sl = slice(h * MEM_DIM, (h + 1) * MEM_DIM)
        sc = _dot_nt(q[:, sl], mk_ref[:, sl].astype(BF16))
        sc = sc - jnp.max(sc, axis=-1, keepdims=True)
        p = jnp.exp(sc)
        p = p / jnp.sum(p, axis=-1, keepdims=True)
        outs.append(_dot(p.astype(BF16), mv_ref[:, sl].astype(BF16)).astype(BF16))
    o = jnp.concatenate(outs, axis=1)
    h2 = DN_ALPHA * x + _dot(o, wo_ref[...])
    o_ref[...] = _layer_norm(h2, g_ref[...], b_ref[...])


def _memory_attention(x1, mem_k, mem_v, wq16, wo16, g, b, *, batch, seq, out_rows, row_offset, prev=None):
    N, D = x1.shape
    M = mem_k.shape[1]
    tm = min(256, seq)
    spb = seq // tm
    off = row_offset // tm
    const = lambda bi, i: (0, 0)
    in_specs = [pl.BlockSpec((tm, D), lambda bi, i: (bi * spb + i, 0)),
                pl.BlockSpec((None, M, D), lambda bi, i: (bi, 0, 0)),
                pl.BlockSpec((None, M, D), lambda bi, i: (bi, 0, 0)),
                pl.BlockSpec((D, D), const), pl.BlockSpec((D, D), const),
                pl.BlockSpec((1, D), const), pl.BlockSpec((1, D), const)]
    args = [x1, mem_k, mem_v, wq16, wo16, g.reshape(1, D), b.reshape(1, D)]
    aliases = {}
    if prev is not None:
        in_specs.append(pl.BlockSpec(memory_space=pl.ANY))
        args.append(prev)
        aliases = {len(args) - 1: 0}
    return pl.pallas_call(
        _memattn_kernel,
        grid=(batch, spb),
        in_specs=in_specs,
        out_specs=pl.BlockSpec((tm, D), lambda bi, i: (off + bi * spb + i, 0)),
        out_shape=jax.ShapeDtypeStruct((out_rows, D), F32),
        input_output_aliases=aliases,
        compiler_params=_params(("parallel", "arbitrary")),
        name="memory_attention_ln2",
    )(*args)


def _router_kernel(x_ref, w_ref, b_ref, tri_ref, o_ref, cnt_ref, run_ref):
    @pl.when(pl.program_id(0) == 0)
    def _():
        run_ref[...] = jnp.zeros(run_ref.shape, F32)

    x = x_ref[...]
    tm = x.shape[0]
    x_hi = x.astype(BF16)
    x_lo = (x - x_hi.astype(F32)).astype(BF16)
    w = w_ref[...]
    w_hi = w.astype(BF16)
    w_lo = (w - w_hi.astype(F32)).astype(BF16)
    logits = _dot(x_hi, w_hi) + _dot(x_lo, w_hi) + _dot(x_hi, w_lo) + b_ref[...]
    lane = lax.broadcasted_iota(jnp.int32, (tm, LANES), 1)
    big = jnp.int32(LANES)
    glog = jnp.where(lane < N_GROUPS, logits, -jnp.inf)
    gmax = jnp.max(glog, axis=-1, keepdims=True)
    grp = jnp.min(jnp.where(glog == gmax, lane, big), axis=-1, keepdims=True)
    g_prob = 1.0 / jnp.sum(jnp.exp(glog - gmax), axis=-1, keepdims=True)
    lo = N_GROUPS + grp * EXPERTS_PER_GROUP
    in_grp = (lane >= lo) & (lane < lo + EXPERTS_PER_GROUP)
    el = jnp.where(in_grp, logits, -jnp.inf)
    v1 = jnp.max(el, axis=-1, keepdims=True)
    i1 = jnp.min(jnp.where(el == v1, lane, big), axis=-1, keepdims=True)
    el2 = jnp.where(lane == i1, -jnp.inf, el)
    v2 = jnp.max(el2, axis=-1, keepdims=True)
    i2 = jnp.min(jnp.where(el2 == v2, lane, big), axis=-1, keepdims=True)
    e2 = jnp.exp(v2 - v1)
    den = 1.0 / (1.0 + e2)
    g1 = den * g_prob
    g2 = e2 * den * g_prob
    hit1 = lane == i1
    hit2 = lane == i2
    member = jnp.where(hit1 | hit2, 1.0, 0.0)
    before = _dot(tri_ref[...], member.astype(BF16)) + run_ref[...]
    r1 = jnp.sum(jnp.where(hit1, before, 0.0), axis=-1, keepdims=True)
    r2 = jnp.sum(jnp.where(hit2, before, 0.0), axis=-1, keepdims=True)
    run = run_ref[...] + jnp.sum(member, axis=0, keepdims=True)
    run_ref[...] = run
    cnt_ref[...] = run
    out = jnp.where(lane == 0, (i1 - N_GROUPS).astype(F32), 0.0)
    out = jnp.where(lane == 1, (i2 - N_GROUPS).astype(F32), out)
    out = jnp.where(lane == 2, g1, out)
    out = jnp.where(lane == 3, g2, out)
    out = jnp.where(lane == 4, r1, out)
    out = jnp.where(lane == 5, r2, out)
    o_ref[...] = out


def _router(x2, w_group, b_group, w_router, b_router):
    N, D = x2.shape
    tm = 256
    w = jnp.zeros((D, LANES), F32).at[:, :N_GROUPS].set(w_group).at[:, N_GROUPS:N_GROUPS + N_EXPERTS].set(w_router)
    b = jnp.zeros((1, LANES), F32).at[0, :N_GROUPS].set(b_group).at[0, N_GROUPS:N_GROUPS + N_EXPERTS].set(b_router)
    r_idx = np.arange(tm)
    strict_lower = jnp.asarray((r_idx[:, None] > r_idx[None, :]).astype(np.float32)).astype(BF16)
    r, cnt = pl.pallas_call(
        _router_kernel,
        grid=(N // tm,),
        in_specs=[pl.BlockSpec((tm, D), lambda i: (i, 0)), pl.BlockSpec((D, LANES), lambda i: (0, 0)),
                  pl.BlockSpec((1, LANES), lambda i: (0, 0)), pl.BlockSpec((tm, tm), lambda i: (0, 0))],
        out_specs=[pl.BlockSpec((tm, LANES), lambda i: (i, 0)), pl.BlockSpec((1, LANES), lambda i: (0, 0))],
        out_shape=[jax.ShapeDtypeStruct((N, LANES), F32), jax.ShapeDtypeStruct((1, LANES), F32)],
        scratch_shapes=[pltpu.VMEM((1, LANES), F32)],
        compiler_params=_params(("arbitrary",)),
        name="router",
    )(x2, w, b, strict_lower)
    expert = r[:, 0:2].astype(jnp.int32)
    gate = r[:, 2:4]
    rank = r[:, 4:6].astype(jnp.int32)
    counts = cnt[0, N_GROUPS:N_GROUPS + N_EXPERTS].astype(jnp.int32)
    return expert, gate, rank, counts


def _row_scatter(idx_ref, k, src_ref, dst_hbm, sem, rows):
    for r in range(rows):
        p = idx_ref[0, k, r]
        pltpu.make_async_copy(src_ref.at[pl.ds(r, 1)], dst_hbm.at[pl.ds(p, 1)], sem).start(priority=r % 2)


def _dispatch_kernel(pos_ref, x_ref, xs_in_hbm, xs_hbm, buf_ref, sem_ref, *, tm, n_tiles):
    del xs_in_hbm
    i = pl.program_id(0)
    slot = i % 2

    def wait(sl):
        for k in range(TOP_K):
            pltpu.make_async_copy(buf_ref.at[sl], xs_hbm.at[pl.ds(0, tm)], sem_ref.at[sl]).wait()

    @pl.when(i >= 2)
    def _():
        wait(slot)

    buf_ref[slot] = x_ref[...]
    for k in range(TOP_K):
        _row_scatter(pos_ref, k, buf_ref.at[slot], xs_hbm, sem_ref.at[slot], tm)

    @pl.when(i == n_tiles - 1)
    def _():
        wait(slot)
        if n_tiles > 1:
            wait(1 - slot)


def _dispatch(x2, pos3, rows):
    N, D = x2.shape
    nt, _, tm = pos3.shape
    xs0 = jnp.zeros((rows, D), F32)
    return pl.pallas_call(
        functools.partial(_dispatch_kernel, tm=tm, n_tiles=nt),
        grid=(nt,),
        in_specs=[pl.BlockSpec((1, TOP_K, tm), lambda i: (i, 0, 0), memory_space=pltpu.SMEM),
                  pl.BlockSpec((tm, D), lambda i: (i, 0)),
                  pl.BlockSpec(memory_space=pl.ANY)],
        out_specs=pl.BlockSpec(memory_space=pl.ANY),
        out_shape=jax.ShapeDtypeStruct((rows, D), F32),
        scratch_shapes=[pltpu.VMEM((2, tm, D), F32), pltpu.SemaphoreType.DMA((2,))],
        input_output_aliases={2: 0},
        compiler_params=_params(("arbitrary",)),
        name="expert_dispatch",
    )(pos3, x2, xs0)


def _row_gather(idx_ref, k, src_hbm, dst_ref, sem, rows):
    for r in range(rows):
        t = idx_ref[0, k, r]
        pltpu.make_async_copy(src_hbm.at[pl.ds(t, 1)], dst_ref.at[pl.ds(r, 1)], sem).start(priority=r % 2)


def _gather_wait(src_hbm, dst_ref, sem, rows):
    pltpu.make_async_copy(src_hbm.at[pl.ds(0, rows)], dst_ref, sem).wait()


def _expert_kernel(be_ref, used_ref, x_ref, wg_ref, wu_ref, wd_ref, o_ref):
    i = pl.program_id(0)

    @pl.when(i < used_ref[0])
    def _():
        xb = x_ref[...].astype(BF16)
        hg = _dot(xb, wg_ref[...])
        hu = _dot(xb, wu_ref[...])
        hid = (hg * jax.nn.sigmoid(hg) * hu).astype(BF16)
        o_ref[...] = _dot(hid, wd_ref[...])

    @pl.when(i >= used_ref[0])
    def _():
        o_ref[...] = jnp.zeros(o_ref.shape, F32)


def _experts(xs, blk_e, n_used, wg16, wu16, wd16):
    R, D = xs.shape
    nb = R // EBLOCK
    grid_spec = pltpu.PrefetchScalarGridSpec(
        num_scalar_prefetch=2,
        grid=(nb,),
        in_specs=[pl.BlockSpec((EBLOCK, D), lambda i, be, nu: (i, 0)),
                  pl.BlockSpec((None, D, D_EXPERT), lambda i, be, nu: (be[i], 0, 0)),
                  pl.BlockSpec((None, D, D_EXPERT), lambda i, be, nu: (be[i], 0, 0)),
                  pl.BlockSpec((None, D_EXPERT, D), lambda i, be, nu: (be[i], 0, 0))],
        out_specs=pl.BlockSpec((EBLOCK, D), lambda i, be, nu: (i, 0)),
    )
    return pl.pallas_call(
        _expert_kernel,
        grid_spec=grid_spec,
        out_shape=jax.ShapeDtypeStruct((R, D), F32),
        compiler_params=_params(("arbitrary",)),
        name="grouped_experts",
    )(blk_e, n_used, xs, wg16, wu16, wd16)


def _combine_kernel(pos_cur_ref, pos_next_ref, x_ref, gate_ref, g_ref, b_ref, y_hbm, oa_ref, ob_ref,
                    buf_ref, sem_ref, *, tm, n_tiles, n_tiles_a):
    i = pl.program_id(0)
    slot = i % 2

    def gather(pos_ref, sl):
        for k in range(TOP_K):
            _row_gather(pos_ref, k, y_hbm, buf_ref.at[sl, k], sem_ref.at[sl], tm)

    def wait(sl):
        for k in range(TOP_K):
            _gather_wait(y_hbm, buf_ref.at[sl, k], sem_ref.at[sl], tm)

    @pl.when(i == 0)
    def _():
        gather(pos_cur_ref, 0)

    wait(slot)
    gather(pos_next_ref, 1 - slot)
    gate = gate_ref[...]
    moe = buf_ref[slot, 0] * gate[:, 0:1] + buf_ref[slot, 1] * gate[:, 1:2]
    h = DN_ALPHA * x_ref[...] + moe
    res = _layer_norm(h, g_ref[...], b_ref[...])

    @pl.when(i < n_tiles_a)
    def _():
        oa_ref[...] = res

    @pl.when(i >= n_tiles_a)
    def _():
        ob_ref[...] = res

    @pl.when(i == n_tiles - 1)
    def _():
        wait(1 - slot)


def _combine(x2, ys, pos3, gate, g, b, rows_a):
    N, D = x2.shape
    nt, _, tm = pos3.shape
    nta = rows_a // tm
    gate_p = jnp.zeros((N, LANES), F32).at[:, 0:TOP_K].set(gate)
    return pl.pallas_call(
        functools.partial(_combine_kernel, tm=tm, n_tiles=nt, n_tiles_a=nta),
        grid=(nt,),
        in_specs=[pl.BlockSpec((1, TOP_K, tm), lambda i: (i, 0, 0), memory_space=pltpu.SMEM),
                  pl.BlockSpec((1, TOP_K, tm), lambda i: (jnp.minimum(i + 1, nt - 1), 0, 0),
                               memory_space=pltpu.SMEM),
                  pl.BlockSpec((tm, D), lambda i: (i, 0)),
                  pl.BlockSpec((tm, LANES), lambda i: (i, 0)),
                  pl.BlockSpec((1, D), lambda i: (0, 0)), pl.BlockSpec((1, D), lambda i: (0, 0)),
                  pl.BlockSpec(memory_space=pl.ANY)],
        out_specs=[pl.BlockSpec((tm, D), lambda i: (jnp.minimum(i, nta - 1), 0)),
                   pl.BlockSpec((tm, D), lambda i: (jnp.maximum(i - nta, 0), 0))],
        out_shape=[jax.ShapeDtypeStruct((rows_a, D), F32), jax.ShapeDtypeStruct((N - rows_a, D), F32)],
        scratch_shapes=[pltpu.VMEM((2, TOP_K, tm, D), F32), pltpu.SemaphoreType.DMA((2,))],
        compiler_params=_params(("arbitrary",)),
        name="combine_ln3",
    )(pos3, pos3, x2, gate_p, g.reshape(1, D), b.reshape(1, D), ys)


def _dispatch_plan(expert, rank, counts):
    N, K = expert.shape
    A = N * K
    E = N_EXPERTS
    padded = (counts + EBLOCK - 1) // EBLOCK * EBLOCK
    pad_end = jnp.cumsum(padded)
    pad_start = pad_end - padded
    n_blocks = (A + E * (EBLOCK - 1) + EBLOCK - 1) // EBLOCK
    ids = jnp.arange(E, dtype=jnp.int32)
    pos = (jnp.sum(jnp.where(expert[..., None] == ids, pad_start, 0), axis=-1) + rank).astype(jnp.int32)
    blk_start = jnp.arange(n_blocks, dtype=jnp.int32) * EBLOCK
    blk_e = jnp.minimum(jnp.sum((pad_end[None, :] <= blk_start[:, None]).astype(jnp.int32), axis=1), E - 1)
    n_used = (pad_end[-1] // EBLOCK).astype(jnp.int32).reshape(1)
    return pos, blk_e.astype(jnp.int32), n_used, n_blocks * EBLOCK


def _hier_moe_ln3(x2, rows_a, w_group, b_group, w_router, b_router, wg16, wu16, wd16, g, b):
    N = x2.shape[0]
    tm = 256
    expert, gate, rank, counts = _router(x2, w_group, b_group, w_router, b_router)
    pos, blk_e, n_used, rows = _dispatch_plan(expert, rank, counts)
    pos3 = pos.reshape(N // tm, tm, TOP_K).transpose(0, 2, 1)
    xs = _dispatch(x2, pos3, rows)
    ys = _experts(xs, blk_e, n_used, wg16, wu16, wd16)
    return _combine(x2, ys, pos3, gate, g, b, rows_a)


def _mixers(x, caches, lam_init, w):
    B, S, D = x.shape
    N = B * S
    xf = x.reshape(N, D)
    past = 0 if caches is None else caches[0].shape[1]
    (ka, va, kb, vb, sga, sgb, qa16, ka16, va16, qb16, kb16, vb16) = _in_projection(x, w['w_in'], past)
    if caches is None:
        tq = min(512, S)
        tk = min(256, S)
        tq_sb = min(1024, S)
        keys = (ka16, va16, kb16, vb16)
    else:
        tq = tq_sb = S
        tk = 256
        views = (caches[0].reshape(B, past, -1), caches[1].reshape(B, past, -1),
                 caches[2].reshape(B, past * DF_HEADS, LANES), caches[3].reshape(B, past * DF_HEADS, LANES))
        keys = tuple(_pack_keys(c, n, past + tk) for c, n in zip(views, (ka16, va16, kb16, vb16)))
    oa = _sb_attention(qa16, keys[0], keys[1], tq=tq_sb, tk=tk, pos0=past)
    tk_df = past + tk if caches is not None else min(1024, S)
    ob = _df_attention(qb16, keys[2], keys[3], w['lam'], w['subln_g'], tq=tq, tk=tk_df, pos0=past,
                       lam_init=lam_init)
    x1 = _merge(xf, oa, ob, sga, sgb, w['w_branch_a'], w['w_branch_b'], w['w_out'], w['ln1_g'], w['ln1_b'])
    new_rows = (ka.reshape(B, S, SB_HEADS, SB_DIM), va.reshape(B, S, SB_HEADS, SB_DIM),
                kb.reshape(B, S, DF_HEADS, 2 * DF_DIM), vb.reshape(B, S, DF_HEADS, 2 * DF_DIM))
    return x1, new_rows


def kernel(x_prompt, mem_prompt, x_sample, cache_sb_k, cache_sb_v, cache_diff_k, cache_diff_v, cache_mem_k, cache_mem_v, w_in, lam_q1, lam_k1, lam_q2, lam_k2, subln_g, w_branch_a, w_branch_b, w_out, ln1_g, ln1_b, w_mq, w_mk, w_mv, w_mo, ln2_g, ln2_b, w_group, b_group, w_router, b_router, w_up, w_gate, w_down, ln3_g, ln3_b):
    depth = w_in.shape[0]
    yp, ys = x_prompt, x_sample
    Bp, n_mem = mem_prompt.shape[0], mem_prompt.shape[1]
    Bs = x_sample.shape[0]
    outs = [[] for _ in range(10)]
    for l in range(depth):
        lam_init = 0.8 - 0.6 * math.exp(-0.3 * l)
        w = {
            'w_in': w_in[l].astype(BF16),
            'lam': jnp.stack([lam_q1[l], lam_k1[l], lam_q2[l], lam_k2[l]]).astype(F32),
            'subln_g': subln_g[l],
            'w_branch_a': w_branch_a[l].astype(BF16), 'w_branch_b': w_branch_b[l].astype(BF16),
            'w_out': w_out[l].astype(BF16), 'ln1_g': ln1_g[l], 'ln1_b': ln1_b[l],
            'w_mq': w_mq[l].astype(BF16), 'w_mo': w_mo[l].astype(BF16),
            'ln2_g': ln2_g[l], 'ln2_b': ln2_b[l],
            'w_group': w_group[l], 'b_group': b_group[l], 'w_router': w_router[l], 'b_router': b_router[l],
            'w_gate': w_gate[l].astype(BF16), 'w_up': w_up[l].astype(BF16), 'w_down': w_down[l].astype(BF16),
            'ln3_g': ln3_g[l], 'ln3_b': ln3_b[l],
        }
        memf = mem_prompt.reshape(Bp * n_mem, D_MODEL)
        mk = _project(memf, w_mk[l].astype(BF16)).reshape(Bp, n_mem, D_MODEL)
        mv = _project(memf, w_mv[l].astype(BF16)).reshape(Bp, n_mem, D_MODEL)
        Np = Bp * yp.shape[1]
        Ns = Bs * ys.shape[1]
        x1p, rows_p = _mixers(yp, None, lam_init, w)
        for lst, val in zip(outs[0:6], rows_p + (mk.reshape(Bp, n_mem, MEM_HEADS, MEM_DIM),
                                               mv.reshape(Bp, n_mem, MEM_HEADS, MEM_DIM))):
            lst.append(val)
        caches = (cache_sb_k[l], cache_sb_v[l], cache_diff_k[l], cache_diff_v[l])
        x1s, rows_s = _mixers(ys, caches, lam_init, w)
        for lst, val in zip(outs[6:10], rows_s):
            lst.append(val)
        x2 = _memory_attention(x1p, mk, mv, w['w_mq'], w['w_mo'], w['ln2_g'], w['ln2_b'],
                               batch=Bp, seq=yp.shape[1], out_rows=Np + Ns, row_offset=0,
                               prev=jnp.zeros((Np + Ns, D_MODEL), F32))
        x2 = _memory_attention(x1s, cache_mem_k[l].reshape(Bs, n_mem, D_MODEL),
                               cache_mem_v[l].reshape(Bs, n_mem, D_MODEL),
                               w['w_mq'], w['w_mo'], w['ln2_g'], w['ln2_b'],
                               batch=Bs, seq=ys.shape[1], out_rows=Np + Ns, row_offset=Np, prev=x2)
        x3p, x3s = _hier_moe_ln3(x2, Np, w['w_group'], w['b_group'], w['w_router'], w['b_router'],
                                 w['w_gate'], w['w_up'], w['w_down'], w['ln3_g'], w['ln3_b'])
        yp = x3p.reshape(yp.shape)
        ys = x3s.reshape(ys.shape)
    return (yp, ys) + tuple(jnp.stack(o) for o in outs)
```

```python
import functools
import math

import jax
import jax.numpy as jnp
import numpy as np
from jax import lax
from jax.experimental import pallas as pl
from jax.experimental.pallas import tpu as pltpu

F32 = jnp.float32
BF16 = jnp.bfloat16

D_MODEL = 1024
SB_HEADS = 8
SB_DIM = 64
SB_W = SB_HEADS * SB_DIM
DF_HEADS = 4
DF_DIM = 64
DF_W = DF_HEADS * 2 * DF_DIM
CHUNK = 64
MEM_HEADS = 4
MEM_DIM = D_MODEL // MEM_HEADS
N_GROUPS = 4
EXPERTS_PER_GROUP = 8
N_EXPERTS = N_GROUPS * EXPERTS_PER_GROUP
TOP_K = 2
D_EXPERT = 512
EBLOCK = 256
ROPE_THETA = 10000.0
LN_EPS = 1e-5
RMS_EPS = 1e-5
DEPTH = 1
DN_ALPHA = (2 * DEPTH) ** 0.25
IN_W = 3 * SB_W + 3 * DF_W + 2 * D_MODEL

LANES = 128
HEAD_GROUPS = 4
LOG2E = 1.4426950408889634
NEG_BIG = -1e30
VMEM_LIMIT = 48 * 1024 * 1024


def _dot(a, b):
    return jnp.dot(a, b, preferred_element_type=F32)


def _dot_nt(a, b):
    return lax.dot_general(a, b, (((1,), (1,)), ((), ())), preferred_element_type=F32)


def _layer_norm(h, g, b):
    mu = jnp.mean(h, axis=-1, keepdims=True)
    d = h - mu
    var = jnp.mean(d * d, axis=-1, keepdims=True)
    return d * lax.rsqrt(var + LN_EPS) * g + b


def _params(sem, vmem=VMEM_LIMIT):
    return pltpu.CompilerParams(dimension_semantics=sem, vmem_limit_bytes=vmem)


def _inproj_kernel(x_ref, w_ref, cos_ref, sina_ref, sinb_ref,
                   ka_ref, va_ref, kb_ref, vb_ref, sga_ref, sgb_ref,
                   qa16_ref, ka16_ref, va16_ref, qb16_ref, kb16_ref, vb16_ref):
    xb = x_ref[...].astype(BF16)

    def mm(off, width):
        return _dot(xb, w_ref[:, off:off + width])

    def put_groups(ref, val):
        for g in range(HEAD_GROUPS):
            ref[g] = val[:, g * LANES:(g + 1) * LANES].astype(BF16)

    lane = lax.broadcasted_iota(jnp.int32, (1, LANES), 1)
    keep = (jnp.where(lane < LANES // 2, 1.0, 0.0), jnp.where(lane >= LANES // 2, 1.0, 0.0))

    def put_halves(ref, val):
        for g in range(HEAD_GROUPS):
            slab = val[:, g * LANES:(g + 1) * LANES]
            for j in range(2):
                ref[2 * g + j] = (slab * keep[j]).astype(BF16)

    def put_head_rows(ref, val):
        for h in range(DF_HEADS):
            ref[pl.ds(h, val.shape[0], stride=DF_HEADS), :] = val[:, h * LANES:(h + 1) * LANES]

    def rope(t):
        n = t.shape[1]
        reps = n // LANES
        cos = jnp.concatenate([cos_ref[...]] * reps, axis=1)
        sina = jnp.concatenate([sina_ref[...]] * reps, axis=1)
        sinb = jnp.concatenate([sinb_ref[...]] * reps, axis=1)
        fwd = pltpu.roll(t, n - DF_DIM // 2, 1)
        bwd = pltpu.roll(t, DF_DIM // 2, 1)
        return t * cos + fwd * sina + bwd * sinb

    qa = mm(0, SB_W)
    put_halves(qa16_ref, qa * (LOG2E * SB_DIM ** -0.5))
    ka = mm(SB_W, SB_W)
    ka_ref[...] = ka
    put_groups(ka16_ref, ka)
    va = mm(2 * SB_W, SB_W)
    va_ref[...] = va
    put_halves(va16_ref, va)
    off = 3 * SB_W
    qb = rope(mm(off, DF_W))
    put_halves(qb16_ref, qb * (LOG2E * DF_DIM ** -0.5))
    kb = rope(mm(off + DF_W, DF_W))
    put_head_rows(kb_ref, kb)
    put_groups(kb16_ref, kb)
    vb = mm(off + 2 * DF_W, DF_W)
    put_head_rows(vb_ref, vb)
    put_groups(vb16_ref, vb)
    off = off + 3 * DF_W
    sga_ref[...] = jax.nn.sigmoid(mm(off, D_MODEL))
    sgb_ref[...] = jax.nn.sigmoid(mm(off + D_MODEL, D_MODEL))


def _rope_tables(pos0, seq):
    half = DF_DIM // 2
    inv_freq = jnp.exp(jnp.arange(half, dtype=F32) * (-math.log(ROPE_THETA) / half))
    ang = (pos0 + jnp.arange(seq, dtype=jnp.int32)).astype(F32)[:, None] * inv_freq[None, :]
    cos, sin = jnp.cos(ang), jnp.sin(ang)
    zero = jnp.zeros_like(sin)
    reps = LANES // DF_DIM
    cos_t = jnp.tile(jnp.concatenate([cos, cos], axis=1), (1, reps))
    sina_t = jnp.tile(jnp.concatenate([-sin, zero], axis=1), (1, reps))
    sinb_t = jnp.tile(jnp.concatenate([zero, sin], axis=1), (1, reps))
    return cos_t, sina_t, sinb_t


def _in_projection(x, w_in16, pos0):
    B, S, D = x.shape
    N = B * S
    tm = min(256, S)
    spb = S // tm
    xf = x.reshape(N, D)
    cos_t, sina_t, sinb_t = _rope_tables(pos0, S)
    row = lambda i: (i, 0)
    tab = lambda i: (i % spb, 0)
    grp = lambda i: (i // spb, 0, i % spb, 0)
    f32_512 = jax.ShapeDtypeStruct((N, SB_W), F32)
    f32_1024 = jax.ShapeDtypeStruct((N, D), F32)
    g16 = jax.ShapeDtypeStruct((B, HEAD_GROUPS, S, LANES), BF16)
    h16 = jax.ShapeDtypeStruct((B, 2 * HEAD_GROUPS, S, LANES), BF16)
    f32_rows = jax.ShapeDtypeStruct((N * DF_HEADS, LANES), F32)
    rspec = pl.BlockSpec((tm * DF_HEADS, LANES), row)
    spec512 = pl.BlockSpec((tm, SB_W), row)
    spec1024 = pl.BlockSpec((tm, D), row)
    gspec = pl.BlockSpec((None, HEAD_GROUPS, tm, LANES), grp)
    hspec = pl.BlockSpec((None, 2 * HEAD_GROUPS, tm, LANES), grp)
    tspec = pl.BlockSpec((tm, LANES), tab)
    return pl.pallas_call(
        _inproj_kernel,
        grid=(N // tm,),
        in_specs=[spec1024, pl.BlockSpec((D, IN_W), lambda i: (0, 0)), tspec, tspec, tspec],
        out_specs=[spec512, spec512, rspec, rspec, spec1024, spec1024,
                   hspec, gspec, hspec, hspec, gspec, gspec],
        out_shape=[f32_512, f32_512, f32_rows, f32_rows, f32_1024, f32_1024,
                   h16, g16, h16, h16, g16, g16],
        compiler_params=_params(("parallel",), 56 * 1024 * 1024),
        name="in_projection",
    )(xf, w_in16, cos_t, sina_t, sinb_t)


def _pack_kernel(c_ref, n_ref, o_ref, *, past, new_rows, halves, head_rows):
    n_slabs = o_ref.shape[0]
    if head_rows:
        for h in range(n_slabs):
            o_ref[h, 0:past, :] = c_ref[pl.ds(h, past, stride=n_slabs), :].astype(BF16)
    else:
        c = c_ref[...]
        lane = lax.broadcasted_iota(jnp.int32, (1, LANES), 1)
        keep = (jnp.where(lane < LANES // 2, 1.0, 0.0), jnp.where(lane >= LANES // 2, 1.0, 0.0))
        for g in range(HEAD_GROUPS):
            slab = c[:, g * LANES:(g + 1) * LANES]
            if halves:
                for j in range(2):
                    o_ref[2 * g + j, 0:past, :] = (slab * keep[j]).astype(BF16)
            else:
                o_ref[g, 0:past, :] = slab.astype(BF16)
    pad = o_ref.shape[1] - past - new_rows
    o_ref[:, past:past + new_rows, :] = n_ref[...]
    o_ref[:, past + new_rows:, :] = jnp.zeros((n_slabs, pad, LANES), BF16)


def _pack_keys(cache, new16, total):
    B, R, W = cache.shape
    n, S = new16.shape[1], new16.shape[2]
    head_rows = W == LANES
    P = R // n if head_rows else R
    return pl.pallas_call(
        functools.partial(_pack_kernel, past=P, new_rows=S, halves=(n == 2 * HEAD_GROUPS), head_rows=head_rows),
        grid=(B,),
        in_specs=[pl.BlockSpec((None, R, W), lambda b: (b, 0, 0)),
                  pl.BlockSpec((None, n, S, LANES), lambda b: (b, 0, 0, 0))],
        out_specs=pl.BlockSpec((None, n, total, LANES), lambda b: (b, 0, 0, 0)),
        out_shape=jax.ShapeDtypeStruct((B, n, total, LANES), BF16),
        compiler_params=_params(("parallel",)),
        name="pack_keys",
    )(cache, new16)


def _schedule(n_q, tq, tk, pos0, variant):
    qi, kb, first, last, masked = [], [], [], [], []
    for i in range(n_q):
        top = -(-(pos0 + (i + 1) * tq) // tk) - 1
        for j in range(top, -1, -1):
            qi.append(i)
            kb.append(j)
            first.append(1 if j == top else 0)
            last.append(1 if j == 0 else 0)
            masked.append(variant(pos0 + i * tq, j * tk, (j + 1) * tk - 1))
    mk = lambda v: jnp.asarray(np.asarray(v, np.int32))
    return mk(qi), mk(kb), mk(first), mk(last), mk(masked)


def _positions(qi, kb, tq, tk, pos0):
    q_pos = pos0 + qi * tq + lax.broadcasted_iota(jnp.int32, (tq, 1), 0)
    k_pos = kb * tk + lax.broadcasted_iota(jnp.int32, (1, tk), 1)
    return q_pos, k_pos


SOFTPLUS_CLAMP = 64.0
STACKED_ROWS_MAX = 512


def _sb_kernel(qi_ref, kb_ref, first_ref, last_ref, masked_ref, q_ref, k_ref, v_ref, tri_ref, o_ref,
               acc_ref, carry_ref, *, tq, tk, pos0):
    stack_heads = SB_HEADS * tq <= STACKED_ROWS_MAX
    trim = min(tk, tq)
    n_masked_variants = 1 if stack_heads else tq // trim
    s = pl.program_id(1)
    qi = qi_ref[s]
    kb = kb_ref[s]

    @pl.when(first_ref[s] == 1)
    def _():
        acc_ref[...] = jnp.zeros(acc_ref.shape, F32)
        carry_ref[...] = jnp.zeros(carry_ref.shape, F32)

    reps = tk // LANES

    def weights(y, mask):
        sp = jnp.maximum(y, jnp.log2(1.0 + jnp.exp2(jnp.minimum(y, SOFTPLUS_CLAMP))))
        if mask is not None:
            sp = jnp.where(mask, sp, 0.0)
        sp16 = sp.astype(BF16)
        later = _dot(sp16, tri_ref[...])
        block_total = later[:, 0:1] + sp16[:, 0:1].astype(F32)
        return sp, later, block_total

    def probs(y, sp, later_total, mask):
        a = jnp.exp2((y - sp) - later_total)
        if mask is not None:
            a = jnp.where(mask, a, 0.0)
        return a.astype(BF16)

    def tile(use_mask, row0=0):
        mask = None
        if stack_heads:
            rows = SB_HEADS * tq
            if use_mask:
                q_pos, k_pos = _positions(qi, kb, tq, tk, pos0)
                mask = jnp.concatenate([k_pos < q_pos] * SB_HEADS, axis=0)
            y = jnp.concatenate([_dot_nt(q_ref[h], k_ref[h // 2]) for h in range(SB_HEADS)], axis=0)
            sp, later, block_total = weights(y, mask)
            carry = carry_ref[...].reshape(rows, LANES)
            a = probs(y, sp, later + jnp.concatenate([carry] * reps, axis=1), mask)
            carry_ref[...] = (carry + jnp.broadcast_to(block_total, (rows, LANES))).reshape(carry_ref.shape)
            for g in range(HEAD_GROUPS):
                upd = (_dot(a[(2 * g) * tq:(2 * g + 1) * tq], v_ref[2 * g])
                       + _dot(a[(2 * g + 1) * tq:(2 * g + 2) * tq], v_ref[2 * g + 1]))
                acc_ref[g] = acc_ref[g] + upd
            return
        n_rows = tq - row0
        live = pl.ds(row0, n_rows)
        if use_mask:
            q_pos, k_pos = _positions(qi, kb, tq, tk, pos0)
            mask = k_pos < q_pos[row0:]

        def group(g, c):
            k = k_ref[g]
            upd = jnp.zeros((n_rows, LANES), F32)
            for j in range(2):
                y = _dot_nt(q_ref[2 * g + j, live, :], k)
                sp, later, block_total = weights(y, mask)
                carry = carry_ref[2 * g + j, live, :]
                a = probs(y, sp, later + jnp.concatenate([carry] * reps, axis=1), mask)
                upd = upd + _dot(a, v_ref[2 * g + j])
                carry_ref[2 * g + j, live, :] = carry + jnp.broadcast_to(block_total, (n_rows, LANES))
            acc_ref[g, live, :] = acc_ref[g, live, :] + upd
            return c

        lax.fori_loop(0, HEAD_GROUPS, group, 0, unroll=True)

    @pl.when(masked_ref[s] == 0)
    def _():
        tile(False)

    for v in range(n_masked_variants):
        @pl.when(masked_ref[s] == v + 1)
        def _(v=v):
            tile(True, v * trim)

    @pl.when(last_ref[s] == 1)
    def _():
        for g in range(HEAD_GROUPS):
            o_ref[:, g * LANES:(g + 1) * LANES] = acc_ref[g].astype(BF16)


def _later_keys_ones(tk):
    r = np.arange(tk)
    return jnp.asarray((r[:, None] > r[None, :]).astype(np.float32)).astype(BF16)


def _sb_attention(q16, k16, v16, *, tq, tk, pos0):
    B, _, S, _ = q16.shape
    n_q = S // tq
    trim = min(tk, tq)
    stacked = SB_HEADS * tq <= STACKED_ROWS_MAX

    def variant(q_first, k_first, k_last):
        if k_last < q_first:
            return 0
        return 1 if stacked else 1 + min(max(k_first - q_first, 0) // trim, tq // trim - 1)

    sched = _schedule(n_q, tq, tk, pos0, variant)
    steps = int(sched[0].shape[0])
    qspec = pl.BlockSpec((None, SB_HEADS, tq, LANES), lambda b, s, qi, kb, f, l, m: (b, 0, qi[s], 0))
    kspec = pl.BlockSpec((None, HEAD_GROUPS, tk, LANES), lambda b, s, qi, kb, f, l, m: (b, 0, kb[s], 0))
    vspec = pl.BlockSpec((None, SB_HEADS, tk, LANES), lambda b, s, qi, kb, f, l, m: (b, 0, kb[s], 0))
    grid_spec = pltpu.PrefetchScalarGridSpec(
        num_scalar_prefetch=5,
        grid=(B, steps),
        in_specs=[qspec, kspec, vspec,
                  pl.BlockSpec((tk, tk), lambda b, s, qi, kb, f, l, m: (0, 0))],
        out_specs=pl.BlockSpec((tq, SB_W), lambda b, s, qi, kb, f, l, m: (b * n_q + qi[s], 0)),
        scratch_shapes=[pltpu.VMEM((HEAD_GROUPS, tq, LANES), F32),
                        pltpu.VMEM((SB_HEADS, tq, LANES), F32)],
    )
    return pl.pallas_call(
        functools.partial(_sb_kernel, tq=tq, tk=tk, pos0=pos0),
        grid_spec=grid_spec,
        out_shape=jax.ShapeDtypeStruct((B * S, SB_W), BF16),
        compiler_params=_params(("parallel", "arbitrary")),
        name="stick_breaking_attention",
    )(*sched, q16, k16, v16, _later_keys_ones(tk))


def _df_kernel(qi_ref, kb_ref, first_ref, last_ref, masked_ref, q_ref, k_ref, v_ref, lam_ref, g_ref,
               o_ref, m_ref, l_ref, acc_ref, *, tq, tk, pos0, lam_init):
    s = pl.program_id(1)
    qi = qi_ref[s]
    kb = kb_ref[s]

    @pl.when(first_ref[s] == 1)
    def _():
        m_ref[...] = jnp.full(m_ref.shape, NEG_BIG, F32)
        l_ref[...] = jnp.zeros(l_ref.shape, F32)
        acc_ref[...] = jnp.zeros(acc_ref.shape, F32)

    reps = tk // LANES

    def tile(use_mask):
        if use_mask:
            q_pos, k_pos = _positions(qi, kb, tq, tk, pos0)
            mask = k_pos <= (q_pos | (CHUNK - 1))

        def head(h, c):
            k = k_ref[h]
            v = v_ref[h]
            for j in range(2):
                idx = 2 * h + j
                sc = _dot_nt(q_ref[idx], k)
                if use_mask:
                    sc = jnp.where(mask, sc, NEG_BIG)
                m_old = m_ref[idx]
                m_new = jnp.maximum(m_old, jnp.max(sc, axis=1, keepdims=True))
                alpha = jnp.exp2(m_old - m_new)
                p = jnp.exp2(sc - jnp.concatenate([m_new] * reps, axis=1))
                l_ref[idx] = alpha * l_ref[idx] + jnp.sum(p, axis=1, keepdims=True)
                acc_ref[idx] = alpha * acc_ref[idx] + _dot(p.astype(BF16), v)
                m_ref[idx] = m_new
            return c

        lax.fori_loop(0, DF_HEADS, head, 0, unroll=True)

    @pl.when(masked_ref[s] == 1)
    def _():
        tile(True)

    @pl.when(masked_ref[s] == 0)
    def _():
        tile(False)

    @pl.when(last_ref[s] == 1)
    def _():
        lv = lam_ref[...]
        lam = (jnp.exp(jnp.sum(lv[0:1] * lv[1:2], axis=1, keepdims=True))
               - jnp.exp(jnp.sum(lv[2:3] * lv[3:4], axis=1, keepdims=True)) + lam_init)
        gain = g_ref[...] * (1.0 - lam_init)
        for h in range(DF_HEADS):
            o = acc_ref[2 * h] / l_ref[2 * h] - lam * (acc_ref[2 * h + 1] / l_ref[2 * h + 1])
            o = o * lax.rsqrt(jnp.mean(o * o, axis=-1, keepdims=True) + RMS_EPS)
            o_ref[:, h * LANES:(h + 1) * LANES] = (o * gain).astype(BF16)


def _df_attention(q16, k16, v16, lam_vecs, subln_g, *, tq, tk, pos0, lam_init):
    B, _, S, _ = q16.shape
    n_q = S // tq
    sched = _schedule(n_q, tq, tk, pos0,
                      lambda q_first, k_first, k_last: 0 if k_last <= (q_first | (CHUNK - 1)) else 1)
    steps = int(sched[0].shape[0])
    qspec = pl.BlockSpec((None, 2 * DF_HEADS, tq, LANES), lambda b, s, qi, kb, f, l, m: (b, 0, qi[s], 0))
    kspec = pl.BlockSpec((None, DF_HEADS, tk, LANES), lambda b, s, qi, kb, f, l, m: (b, 0, kb[s], 0))
    const = lambda b, s, qi, kb, f, l, m: (0, 0)
    grid_spec = pltpu.PrefetchScalarGridSpec(
        num_scalar_prefetch=5,
        grid=(B, steps),
        in_specs=[qspec, kspec, kspec,
                  pl.BlockSpec((4, DF_DIM), const), pl.BlockSpec((1, LANES), const)],
        out_specs=pl.BlockSpec((tq, DF_W), lambda b, s, qi, kb, f, l, m: (b * n_q + qi[s], 0)),
        scratch_shapes=[pltpu.VMEM((2 * DF_HEADS, tq, LANES), F32),
                        pltpu.VMEM((2 * DF_HEADS, tq, LANES), F32),
                        pltpu.VMEM((2 * DF_HEADS, tq, LANES), F32)],
    )
    return pl.pallas_call(
        functools.partial(_df_kernel, tq=tq, tk=tk, pos0=pos0, lam_init=lam_init),
        grid_spec=grid_spec,
        out_shape=jax.ShapeDtypeStruct((B * S, DF_W), BF16),
        compiler_params=_params(("parallel", "arbitrary")),
        name="differential_attention",
    )(*sched, q16, k16, v16, lam_vecs, subln_g.reshape(1, LANES))


def _merge_kernel(x_ref, oa_ref, ob_ref, sga_ref, sgb_ref, wa_ref, wb_ref, wo_ref, g_ref, b_ref, o_ref):
    ya = _dot(oa_ref[...], wa_ref[...])
    yb = _dot(ob_ref[...], wb_ref[...])
    mix = (sga_ref[...] * ya + sgb_ref[...] * yb).astype(BF16)
    h = DN_ALPHA * x_ref[...] + _dot(mix, wo_ref[...])
    o_ref[...] = _layer_norm(h, g_ref[...], b_ref[...])


def _merge(xf, oa, ob, sga, sgb, wa16, wb16, wo16, g, b):
    N, D = xf.shape
    tm = 256
    row = lambda i: (i, 0)
    const = lambda i: (0, 0)
    return pl.pallas_call(
        _merge_kernel,
        grid=(N // tm,),
        in_specs=[pl.BlockSpec((tm, D), row), pl.BlockSpec((tm, SB_W), row), pl.BlockSpec((tm, DF_W), row),
                  pl.BlockSpec((tm, D), row), pl.BlockSpec((tm, D), row),
                  pl.BlockSpec((SB_W, D), const), pl.BlockSpec((DF_W, D), const), pl.BlockSpec((D, D), const),
                  pl.BlockSpec((1, D), const), pl.BlockSpec((1, D), const)],
        out_specs=pl.BlockSpec((tm, D), row),
        out_shape=jax.ShapeDtypeStruct((N, D), F32),
        compiler_params=_params(("parallel",)),
        name="merge_ln1",
    )(xf, oa, ob, sga, sgb, wa16, wb16, wo16, g.reshape(1, D), b.reshape(1, D))


def _proj_kernel(x_ref, w_ref, o_ref):
    o_ref[...] = _dot(x_ref[...].astype(BF16), w_ref[...])


def _project(xf, w16):
    N, D = xf.shape
    tm = 256
    return pl.pallas_call(
        _proj_kernel,
        grid=(N // tm,),
        in_specs=[pl.BlockSpec((tm, D), lambda i: (i, 0)), pl.BlockSpec(w16.shape, lambda i: (0, 0))],
        out_specs=pl.BlockSpec((tm, w16.shape[1]), lambda i: (i, 0)),
        out_shape=jax.ShapeDtypeStruct((N, w16.shape[1]), F32),
        compiler_params=_params(("parallel",)),
        name="memory_projection",
    )(xf, w16)


def _memattn_kernel(x_ref, mk_ref, mv_ref, wq_ref, wo_ref, g_ref, b_ref, *rest):
    o_ref = rest[-1]
    x = x_ref[...]
    q = (_dot(x.astype(BF16), wq_ref[...]) * (MEM_DIM ** -0.5)).astype(BF16)
    outs = []
    for h in range(MEM_HEADS):
        sl = slice(h * MEM_DIM, (h + 1) * MEM_DIM)
        sc = _dot_nt(q[:, sl], mk_ref[:, sl].astype(BF16))
        sc = sc - jnp.max(sc, axis=-1, keepdims=True)
        p = jnp.exp(sc)
        p = p / jnp.sum(p, axis=-1, keepdims=True)
        outs.append(_dot(p.astype(BF16), mv_ref[:, sl].astype(BF16)).astype(BF16))
    o = jnp.concatenate(outs, axis=1)
    h2 = DN_ALPHA * x + _dot(o, wo_ref[...])
    o_ref[...] = _layer_norm(h2, g_ref[...], b_ref[...])


def _memory_attention(x1, mem_k, mem_v, wq16, wo16, g, b, *, batch, seq, out_rows, row_offset, prev=None):
    N, D = x1.shape
    M = mem_k.shape[1]
    tm = min(256, seq)
    spb = seq // tm
    off = row_offset // tm
    const = lambda bi, i: (0, 0)
    in_specs = [pl.BlockSpec((tm, D), lambda bi, i: (bi * spb + i, 0)),
                pl.BlockSpec((None, M, D), lambda bi, i: (bi, 0, 0)),
                pl.BlockSpec((None, M, D), lambda bi, i: (bi, 0, 0)),
                pl.BlockSpec((D, D), const), pl.BlockSpec((D, D), const),
                pl.BlockSpec((1, D), const), pl.BlockSpec((1, D), const)]
    args = [x1, mem_k, mem_v, wq16, wo16, g.reshape(1, D), b.reshape(1, D)]
    aliases = {}
    if prev is not None:
        in_specs.append(pl.BlockSpec(memory_space=pl.ANY))
        args.append(prev)
        aliases = {len(args) - 1: 0}
    return pl.pallas_call(
        _memattn_kernel,
        grid=(batch, spb),
        in_specs=in_specs,
        out_specs=pl.BlockSpec((tm, D), lambda bi, i: (off + bi * spb + i, 0)),
        out_shape=jax.ShapeDtypeStruct((out_rows, D), F32),
        input_output_aliases=aliases,
        compiler_params=_params(("parallel", "arbitrary")),
        name="memory_attention_ln2",
    )(*args)


def _router_kernel(x_ref, w_ref, b_ref, tri_ref, o_ref, cnt_ref, run_ref):
    @pl.when(pl.program_id(0) == 0)
    def _():
        run_ref[...] = jnp.zeros(run_ref.shape, F32)

    x = x_ref[...]
    tm = x.shape[0]
    x_hi = x.astype(BF16)
    x_lo = (x - x_hi.astype(F32)).astype(BF16)
    w = w_ref[...]
    w_hi = w.astype(BF16)
    w_lo = (w - w_hi.astype(F32)).astype(BF16)
    logits = _dot(x_hi, w_hi) + _dot(x_lo, w_hi) + _dot(x_hi, w_lo) + b_ref[...]
    lane = lax.broadcasted_iota(jnp.int32, (tm, LANES), 1)
    big = jnp.int32(LANES)
    glog = jnp.where(lane < N_GROUPS, logits, -jnp.inf)
    gmax = jnp.max(glog, axis=-1, keepdims=True)
    grp = jnp.min(jnp.where(glog == gmax, lane, big), axis=-1, keepdims=True)
    g_prob = 1.0 / jnp.sum(jnp.exp(glog - gmax), axis=-1, keepdims=True)
    lo = N_GROUPS + grp * EXPERTS_PER_GROUP
    in_grp = (lane >= lo) & (lane < lo + EXPERTS_PER_GROUP)
    el = jnp.where(in_grp, logits, -jnp.inf)
    v1 = jnp.max(el, axis=-1, keepdims=True)
    i1 = jnp.min(jnp.where(el == v1, lane, big), axis=-1, keepdims=True)
    el2 = jnp.where(lane == i1, -jnp.inf, el)
    v2 = jnp.max(el2, axis=-1, keepdims=True)
    i2 = jnp.min(jnp.where(el2 == v2, lane, big), axis=-1, keepdims=True)
    e2 = jnp.exp(v2 - v1)
    den = 1.0 / (1.0 + e2)
    g1 = den * g_prob
    g2 = e2 * den * g_prob
    hit1 = lane == i1
    hit2 = lane == i2
    member = jnp.where(hit1 | hit2, 1.0, 0.0)
    before = _dot(tri_ref[...], member.astype(BF16)) + run_ref[...]
    r1 = jnp.sum(jnp.where(hit1, before, 0.0), axis=-1, keepdims=True)
    r2 = jnp.sum(jnp.where(hit2, before, 0.0), axis=-1, keepdims=True)
    run = run_ref[...] + jnp.sum(member, axis=0, keepdims=True)
    run_ref[...] = run
    cnt_ref[...] = run
    out = jnp.where(lane == 0, (i1 - N_GROUPS).astype(F32), 0.0)
    out = jnp.where(lane == 1, (i2 - N_GROUPS).astype(F32), out)
    out = jnp.where(lane == 2, g1, out)
    out = jnp.where(lane == 3, g2, out)
    out = jnp.where(lane == 4, r1, out)
    out = jnp.where(lane == 5, r2, out)
    o_ref[...] = out


def _router(x2, w_group, b_group, w_router, b_router):
    N, D = x2.shape
    tm = 256
    w = jnp.zeros((D, LANES), F32).at[:, :N_GROUPS].set(w_group).at[:, N_GROUPS:N_GROUPS + N_EXPERTS].set(w_router)
    b = jnp.zeros((1, LANES), F32).at[0, :N_GROUPS].set(b_group).at[0, N_GROUPS:N_GROUPS + N_EXPERTS].set(b_router)
    r_idx = np.arange(tm)
    strict_lower = jnp.asarray((r_idx[:, None] > r_idx[None, :]).astype(np.float32)).astype(BF16)
    r, cnt = pl.pallas_call(
        _router_kernel,
        grid=(N // tm,),
        in_specs=[pl.BlockSpec((tm, D), lambda i: (i, 0)), pl.BlockSpec((D, LANES), lambda i: (0, 0)),
                  pl.BlockSpec((1, LANES), lambda i: (0, 0)), pl.BlockSpec((tm, tm), lambda i: (0, 0))],
        out_specs=[pl.BlockSpec((tm, LANES), lambda i: (i, 0)), pl.BlockSpec((1, LANES), lambda i: (0, 0))],
        out_shape=[jax.ShapeDtypeStruct((N, LANES), F32), jax.ShapeDtypeStruct((1, LANES), F32)],
        scratch_shapes=[pltpu.VMEM((1, LANES), F32)],
        compiler_params=_params(("arbitrary",)),
        name="router",
    )(x2, w, b, strict_lower)
    expert = r[:, 0:2].astype(jnp.int32)
    gate = r[:, 2:4]
    rank = r[:, 4:6].astype(jnp.int32)
    counts = cnt[0, N_GROUPS:N_GROUPS + N_EXPERTS].astype(jnp.int32)
    return expert, gate, rank, counts


def _row_scatter(idx_ref, k, src_ref, dst_hbm, sem, rows):
    for r in range(rows):
        p = idx_ref[0, k, r]
        pltpu.make_async_copy(src_ref.at[pl.ds(r, 1)], dst_hbm.at[pl.ds(p, 1)], sem).start(priority=r % 2)


def _dispatch_kernel(pos_ref, x_ref, xs_in_hbm, xs_hbm, buf_ref, sem_ref, *, tm, n_tiles):
    del xs_in_hbm
    i = pl.program_id(0)
    slot = i % 2

    def wait(sl):
        for k in range(TOP_K):
            pltpu.make_async_copy(buf_ref.at[sl], xs_hbm.at[pl.ds(0, tm)], sem_ref.at[sl]).wait()

    @pl.when(i >= 2)
    def _():
        wait(slot)

    buf_ref[slot] = x_ref[...]
    for k in range(TOP_K):
        _row_scatter(pos_ref, k, buf_ref.at[slot], xs_hbm, sem_ref.at[slot], tm)

    @pl.when(i == n_tiles - 1)
    def _():
        wait(slot)
        if n_tiles > 1:
            wait(1 - slot)


def _dispatch(x2, pos3, rows):
    N, D = x2.shape
    nt, _, tm = pos3.shape
    xs0 = jnp.zeros((rows, D), F32)
    return pl.pallas_call(
        functools.partial(_dispatch_kernel, tm=tm, n_tiles=nt),
        grid=(nt,),
        in_specs=[pl.BlockSpec((1, TOP_K, tm), lambda i: (i, 0, 0), memory_space=pltpu.SMEM),
                  pl.BlockSpec((tm, D), lambda i: (i, 0)),
                  pl.BlockSpec(memory_space=pl.ANY)],
        out_specs=pl.BlockSpec(memory_space=pl.ANY),
        out_shape=jax.ShapeDtypeStruct((rows, D), F32),
        scratch_shapes=[pltpu.VMEM((2, tm, D), F32), pltpu.SemaphoreType.DMA((2,))],
        input_output_aliases={2: 0},
        compiler_params=_params(("arbitrary",)),
        name="expert_dispatch",
    )(pos3, x2, xs0)


def _row_gather(idx_ref, k, src_hbm, dst_ref, sem, rows):
    for r in range(rows):
        t = idx_ref[0, k, r]
        pltpu.make_async_copy(src_hbm.at[pl.ds(t, 1)], dst_ref.at[pl.ds(r, 1)], sem).start(priority=r % 2)


def _gather_wait(src_hbm, dst_ref, sem, rows):
    pltpu.make_async_copy(src_hbm.at[pl.ds(0, rows)], dst_ref, sem).wait()


def _expert_kernel(be_ref, used_ref, x_ref, wg_ref, wu_ref, wd_ref, o_ref):
    i = pl.program_id(0)

    @pl.when(i < used_ref[0])
    def _():
        xb = x_ref[...].astype(BF16)
        hg = _dot(xb, wg_ref[...])
        hu = _dot(xb, wu_ref[...])
        hid = (hg * jax.nn.sigmoid(hg) * hu).astype(BF16)
        o_ref[...] = _dot(hid, wd_ref[...])

    @pl.when(i >= used_ref[0])
    def _():
        o_ref[...] = jnp.zeros(o_ref.shape, F32)


def _experts(xs, blk_e, n_used, wg16, wu16, wd16):
    R, D = xs.shape
    nb = R // EBLOCK
    grid_spec = pltpu.PrefetchScalarGridSpec(
        num_scalar_prefetch=2,
        grid=(nb,),
        in_specs=[pl.BlockSpec((EBLOCK, D), lambda i, be, nu: (i, 0)),
                  pl.BlockSpec((None, D, D_EXPERT), lambda i, be, nu: (be[i], 0, 0)),
                  pl.BlockSpec((None, D, D_EXPERT), lambda i, be, nu: (be[i], 0, 0)),
                  pl.BlockSpec((None, D_EXPERT, D), lambda i, be, nu: (be[i], 0, 0))],
        out_specs=pl.BlockSpec((EBLOCK, D), lambda i, be, nu: (i, 0)),
    )
    return pl.pallas_call(
        _expert_kernel,
        grid_spec=grid_spec,
        out_shape=jax.ShapeDtypeStruct((R, D), F32),
        compiler_params=_params(("arbitrary",)),
        name="grouped_experts",
    )(blk_e, n_used, xs, wg16, wu16, wd16)


def _combine_kernel(pos_cur_ref, pos_next_ref, x_ref, gate_ref, g_ref, b_ref, y_hbm, oa_ref, ob_ref,
                    buf_ref, sem_ref, *, tm, n_tiles, n_tiles_a):
    i = pl.program_id(0)
    slot = i % 2

    def gather(pos_ref, sl):
        for k in range(TOP_K):
            _row_gather(pos_ref, k, y_hbm, buf_ref.at[sl, k], sem_ref.at[sl], tm)

    def wait(sl):
        for k in range(TOP_K):
            _gather_wait(y_hbm, buf_ref.at[sl, k], sem_ref.at[sl], tm)

    @pl.when(i == 0)
    def _():
        gather(pos_cur_ref, 0)

    wait(slot)
    gather(pos_next_ref, 1 - slot)
    gate = gate_ref[...]
    moe = buf_ref[slot, 0] * gate[:, 0:1] + buf_ref[slot, 1] * gate[:, 1:2]
    h = DN_ALPHA * x_ref[...] + moe
    res = _layer_norm(h, g_ref[...], b_ref[...])

    @pl.when(i < n_tiles_a)
    def _():
        oa_ref[...] = res

    @pl.when(i >= n_tiles_a)
    def _():
        ob_ref[...] = res

    @pl.when(i == n_tiles - 1)
    def _():
        wait(1 - slot)


def _combine(x2, ys, pos3, gate, g, b, rows_a):
    N, D = x2.shape
    nt, _, tm = pos3.shape
    nta = rows_a // tm
    gate_p = jnp.zeros((N, LANES), F32).at[:, 0:TOP_K].set(gate)
    return pl.pallas_call(
        functools.partial(_combine_kernel, tm=tm, n_tiles=nt, n_tiles_a=nta),
        grid=(nt,),
        in_specs=[pl.BlockSpec((1, TOP_K, tm), lambda i: (i, 0, 0), memory_space=pltpu.SMEM),
                  pl.BlockSpec((1, TOP_K, tm), lambda i: (jnp.minimum(i + 1, nt - 1), 0, 0),
                               memory_space=pltpu.SMEM),
                  pl.BlockSpec((tm, D), lambda i: (i, 0)),
                  pl.BlockSpec((tm, LANES), lambda i: (i, 0)),
                  pl.BlockSpec((1, D), lambda i: (0, 0)), pl.BlockSpec((1, D), lambda i: (0, 0)),
                  pl.BlockSpec(memory_space=pl.ANY)],
        out_specs=[pl.BlockSpec((tm, D), lambda i: (jnp.minimum(i, nta - 1), 0)),
                   pl.BlockSpec((tm, D), lambda i: (jnp.maximum(i - nta, 0), 0))],
        out_shape=[jax.ShapeDtypeStruct((rows_a, D), F32), jax.ShapeDtypeStruct((N - rows_a, D), F32)],
        scratch_shapes=[pltpu.VMEM((2, TOP_K, tm, D), F32), pltpu.SemaphoreType.DMA((2,))],
        compiler_params=_params(("arbitrary",)),
        name="combine_ln3",
    )(pos3, pos3, x2, gate_p, g.reshape(1, D), b.reshape(1, D), ys)


def _dispatch_plan(expert, rank, counts):
    N, K = expert.shape
    A = N * K
    E = N_EXPERTS
    padded = (counts + EBLOCK - 1) // EBLOCK * EBLOCK
    pad_end = jnp.cumsum(padded)
    pad_start = pad_end - padded
    n_blocks = (A + E * (EBLOCK - 1) + EBLOCK - 1) // EBLOCK
    ids = jnp.arange(E, dtype=jnp.int32)
    pos = (jnp.sum(jnp.where(expert[..., None] == ids, pad_start, 0), axis=-1) + rank).astype(jnp.int32)
    blk_start = jnp.arange(n_blocks, dtype=jnp.int32) * EBLOCK
    blk_e = jnp.minimum(jnp.sum((pad_end[None, :] <= blk_start[:, None]).astype(jnp.int32), axis=1), E - 1)
    n_used = (pad_end[-1] // EBLOCK).astype(jnp.int32).reshape(1)
    return pos, blk_e.astype(jnp.int32), n_used, n_blocks * EBLOCK


def _hier_moe_ln3(x2, rows_a, w_group, b_group, w_router, b_router, wg16, wu16, wd16, g, b):
    N = x2.shape[0]
    tm = 256
    expert, gate, rank, counts = _router(x2, w_group, b_group, w_router, b_router)
    pos, blk_e, n_used, rows = _dispatch_plan(expert, rank, counts)
    pos3 = pos.reshape(N // tm, tm, TOP_K).transpose(0, 2, 1)
    xs = _dispatch(x2, pos3, rows)
    ys = _experts(xs, blk_e, n_used, wg16, wu16, wd16)
    return _combine(x2, ys, pos3, gate, g, b, rows_a)


def _mixers(x, caches, lam_init, w):
    B, S, D = x.shape
    N = B * S
    xf = x.reshape(N, D)
    past = 0 if caches is None else caches[0].shape[1]
    (ka, va, kb, vb, sga, sgb, qa16, ka16, va16, qb16, kb16, vb16) = _in_projection(x, w['w_in'], past)
    if caches is None:
        tq = min(512, S)
        tk = min(256, S)
        tq_sb = min(1024, S)
        keys = (ka16, va16, kb16, vb16)
    else:
        tq = tq_sb = S
        tk = 256
        views = (caches[0].reshape(B, past, -1), caches[1].reshape(B, past, -1),
                 caches[2].reshape(B, past * DF_HEADS, LANES), caches[3].reshape(B, past * DF_HEADS, LANES))
        keys = tuple(_pack_keys(c, n, past + tk) for c, n in zip(views, (ka16, va16, kb16, vb16)))
    oa = _sb_attention(qa16, keys[0], keys[1], tq=tq_sb, tk=tk, pos0=past)
    tk_df = past + tk if caches is not None else min(1024, S)
    ob = _df_attention(qb16, keys[2], keys[3], w['lam'], w['subln_g'], tq=tq, tk=tk_df, pos0=past,
                       lam_init=lam_init)
    x1 = _merge(xf, oa, ob, sga, sgb, w['w_branch_a'], w['w_branch_b'], w['w_out'], w['ln1_g'], w['ln1_b'])
    new_rows = (ka.reshape(B, S, SB_HEADS, SB_DIM), va.reshape(B, S, SB_HEADS, SB_DIM),
                kb.reshape(B, S, DF_HEADS, 2 * DF_DIM), vb.reshape(B, S, DF_HEADS, 2 * DF_DIM))
    return x1, new_rows


def kernel(x_prompt, mem_prompt, x_sample, cache_sb_k, cache_sb_v, cache_diff_k, cache_diff_v, cache_mem_k, cache_mem_v, w_in, lam_q1, lam_k1, lam_q2, lam_k2, subln_g, w_branch_a, w_branch_b, w_out, ln1_g, ln1_b, w_mq, w_mk, w_mv, w_mo, ln2_g, ln2_b, w_group, b_group, w_router, b_router, w_up, w_gate, w_down, ln3_g, ln3_b):
    depth = w_in.shape[0]
    yp, ys = x_prompt, x_sample
    Bp, n_mem = mem_prompt.shape[0], mem_prompt.shape[1]
    Bs = x_sample.shape[0]
    outs = [[] for _ in range(10)]
    for l in range(depth):
        lam_init = 0.8 - 0.6 * math.exp(-0.3 * l)
        w = {
            'w_in': w_in[l].astype(BF16),
            'lam': jnp.stack([lam_q1[l], lam_k1[l], lam_q2[l], lam_k2[l]]).astype(F32),
            'subln_g': subln_g[l],
            'w_branch_a': w_branch_a[l].astype(BF16), 'w_branch_b': w_branch_b[l].astype(BF16),
            'w_out': w_out[l].astype(BF16), 'ln1_g': ln1_g[l], 'ln1_b': ln1_b[l],
            'w_mq': w_mq[l].astype(BF16), 'w_mo': w_mo[l].astype(BF16),
            'ln2_g': ln2_g[l], 'ln2_b': ln2_b[l],
            'w_group': w_group[l], 'b_group': b_group[l], 'w_router': w_router[l], 'b_router': b_router[l],
            'w_gate': w_gate[l].astype(BF16), 'w_up': w_up[l].astype(BF16), 'w_down': w_down[l].astype(BF16),
            'ln3_g': ln3_g[l], 'ln3_b': ln3_b[l],
        }
        memf = mem_prompt.reshape(Bp * n_mem, D_MODEL)
        mk = _project(memf, w_mk[l].astype(BF16)).reshape(Bp, n_mem, D_MODEL)
        mv = _project(memf, w_mv[l].astype(BF16)).reshape(Bp, n_mem, D_MODEL)
        Np = Bp * yp.shape[1]
        Ns = Bs * ys.shape[1]
        x1p, rows_p = _mixers(yp, None, lam_init, w)
        for lst, val in zip(outs[0:6], rows_p + (mk.reshape(Bp, n_mem, MEM_HEADS, MEM_DIM),
                                               mv.reshape(Bp, n_mem, MEM_HEADS, MEM_DIM))):
            lst.append(val)
        caches = (cache_sb_k[l], cache_sb_v[l], cache_diff_k[l], cache_diff_v[l])
        x1s, rows_s = _mixers(ys, caches, lam_init, w)
        for lst, val in zip(outs[6:10], rows_s):
            lst.append(val)
        x2 = _memory_attention(x1p, mk, mv, w['w_mq'], w['w_mo'], w['ln2_g'], w['ln2_b'],
                               batch=Bp, seq=yp.shape[1], out_rows=Np + Ns, row_offset=0,
                               prev=jnp.zeros((Np + Ns, D_MODEL), F32))
        x2 = _memory_attention(x1s, cache_mem_k[l].reshape(Bs, n_mem, D_MODEL),
                               cache_mem_v[l].reshape(Bs, n_mem, D_MODEL),
                               w['w_mq'], w['w_mo'], w['ln2_g'], w['ln2_b'],
                               batch=Bs, seq=ys.shape[1], out_rows=Np + Ns, row_offset=Np, prev=x2)
        x3p, x3s = _hier_moe_ln3(x2, Np, w['w_group'], w['b_group'], w['w_router'], w['b_router'],
                                 w['w_gate'], w['w_up'], w['w_down'], w['ln3_g'], w['ln3_b'])
        yp = x3p.reshape(yp.shape)
        ys = x3s.reshape(ys.shape)
    return (yp, ys) + tuple(jnp.stack(o) for o in outs)
```
